```python
import jax, jax.numpy as jnp
from jax import lax
import numpy as np

D_MODEL = 4096
BATCH = 4
SEQ = 4096
DEPTH = 1

N_MEM = 256
EPS = 1e-6

GLA_HEADS = 4
GLA_DK = 256
GLA_DV = 512
GLA_RANK = 16
GLA_TAU = 16.0
GLA_CHUNK = 16
GLA_QK = GLA_HEADS * GLA_DK
GLA_V = GLA_HEADS * GLA_DV

CONV_WIDTH = 2048
CONV_K = 3

XA_HEADS = 4
XA_DH = 512
XA_W = XA_HEADS * XA_DH

N_BRANCH = 3

PEER_HEADS = 8
PEER_NKEYS = 128
PEER_DQ = 256
PEER_TOPK = 16
PEER_EXPERTS = PEER_NKEYS * PEER_NKEYS
PEER_BLOCK = 128

SPLIT_SIZES = (GLA_QK, GLA_QK, GLA_V, GLA_V, GLA_RANK, CONV_WIDTH, CONV_WIDTH, CONV_WIDTH, XA_W, N_BRANCH * D_MODEL)
IN_WIDTH = sum(SPLIT_SIZES)

kernel_name = "hybrid_gla_shortconv_memxattn_peer"


def rms_norm(x, g):
    xf = x.astype(jnp.float32)
    y = xf * lax.rsqrt(jnp.mean(xf * xf, axis=-1, keepdims=True) + EPS)
    return (y * g.astype(jnp.float32)).astype(x.dtype)


def gla_chunked(q, k, v, log_a):
    B, S, H, dk = q.shape
    dv = v.shape[-1]
    n = S // GLA_CHUNK

    def to_chunks(t):
        return t.reshape(B, n, GLA_CHUNK, H, t.shape[-1]).transpose(1, 0, 3, 2, 4).astype(jnp.float32)

    qc, kc, vc, ac = to_chunks(q), to_chunks(k), to_chunks(v), to_chunks(log_a)
    b = jnp.cumsum(ac, axis=3)
    b_last = b[:, :, :, -1:, :]
    q_t = qc * jnp.exp(b) * (dk ** -0.5)
    k_in = kc * jnp.exp(-b)
    k_out = kc * jnp.exp(b_last - b)
    decay = jnp.exp(b_last[:, :, :, 0, :])

    causal = jnp.tril(jnp.ones((GLA_CHUNK, GLA_CHUNK), dtype=bool))
    att = jnp.einsum('nbhtd,nbhsd->nbhts', q_t, k_in)
    att = jnp.where(causal, att, 0.0)
    o_intra = jnp.einsum('nbhts,nbhsv->nbhtv', att, vc)

    def step(state, inp):
        q_i, k_i, v_i, d_i = inp
        o = jnp.einsum('bhtd,bhdv->bhtv', q_i, state)
        state = d_i[..., None] * state + jnp.einsum('bhsd,bhsv->bhdv', k_i, v_i)
        return state, o

    s0 = jnp.zeros((B, H, dk, dv), jnp.float32)
    _, o_inter = lax.scan(step, s0, (q_t, k_out, vc, decay))
    o = (o_intra + o_inter).transpose(1, 0, 3, 2, 4).reshape(B, S, H, dv)
    return o


def causal_short_conv(u, w):
    S = u.shape[1]
    up = jnp.pad(u, ((0, 0), (CONV_K - 1, 0), (0, 0)))
    y = w[0] * up[:, CONV_K - 1:CONV_K - 1 + S]
    for j in range(1, CONV_K):
        y = y + w[j] * up[:, CONV_K - 1 - j:CONV_K - 1 - j + S]
    return y


def memory_cross_attention(q, mem_kv):
    B, S, _ = q.shape
    M = mem_kv.shape[1]
    qh = q.reshape(B, S, XA_HEADS, XA_DH)
    kh = mem_kv[..., :XA_W].reshape(B, M, XA_HEADS, XA_DH)
    vh = mem_kv[..., XA_W:].reshape(B, M, XA_HEADS, XA_DH)
    s = jnp.einsum('bshd,bmhd->bhsm', qh, kh).astype(jnp.float32) * (XA_DH ** -0.5)
    p = jax.nn.softmax(s, axis=-1).astype(vh.dtype)
    o = jnp.einsum('bhsm,bmhd->bshd', p, vh)
    return o.reshape(B, S, XA_W)


def peer(h, w_q, sub_keys, u_tab, v_tab):
    B, S, D = h.shape
    T = B * S
    hf = h.reshape(T, D)
    q = (hf @ w_q).reshape(T, PEER_HEADS, 2, PEER_DQ // 2)
    s = jnp.einsum('thcd,hckd->thck', q, sub_keys).astype(jnp.float32)
    sv, si = lax.top_k(s, PEER_TOPK)
    cand = sv[:, :, 0, :, None] + sv[:, :, 1, None, :]
    cand_idx = si[:, :, 0, :, None] * PEER_NKEYS + si[:, :, 1, None, :]
    cv, ci = lax.top_k(cand.reshape(T, PEER_HEADS, PEER_TOPK * PEER_TOPK), PEER_TOPK)
    eidx = jnp.take_along_axis(cand_idx.reshape(T, PEER_HEADS, PEER_TOPK * PEER_TOPK), ci, axis=-1)
    g = jax.nn.softmax(cv, axis=-1)

    nb = T // PEER_BLOCK
    xb = hf.reshape(nb, PEER_BLOCK, D)
    ib = eidx.reshape(nb, PEER_BLOCK, PEER_HEADS * PEER_TOPK)
    gb = g.reshape(nb, PEER_BLOCK, PEER_HEADS * PEER_TOPK)

    def block(args):
        x_i, idx_i, g_i = args
        u_sel = u_tab[idx_i]
        a = jnp.einsum('tkd,td->tk', u_sel, x_i).astype(jnp.float32)
        wgt = (g_i * jax.nn.gelu(a, approximate=False)).astype(v_tab.dtype)
        v_sel = v_tab[idx_i]
        return jnp.einsum('tk,tkd->td', wgt, v_sel)

    out = lax.map(block, (xb, ib, gb))
    return out.reshape(B, S, D).astype(h.dtype)


def setup_inputs(seed: int = 0) -> dict:
    key = jax.random.key(seed)
    ks = jax.random.split(key, 24)
    f32 = jnp.float32
    D = D_MODEL

    def nrm(k, shape, scale):
        return jax.random.normal(k, shape, f32) * scale

    def gain(k, shape):
        return 1.0 + 0.02 * jax.random.normal(k, shape, f32)

    return {
        "x": jax.random.normal(ks[0], (BATCH, SEQ, D), f32),
        "mem": jax.random.normal(ks[1], (BATCH, N_MEM, D), f32),
        "norm_mix": gain(ks[2], (DEPTH, D)),
        "w_in": nrm(ks[3], (DEPTH, D, IN_WIDTH), D ** -0.5),
        "w_a_up": nrm(ks[4], (DEPTH, GLA_RANK, GLA_QK), GLA_RANK ** -0.5),
        "b_a": nrm(ks[5], (DEPTH, GLA_QK), 0.1),
        "gla_norm": gain(ks[6], (DEPTH, GLA_DV)),
        "conv_w": nrm(ks[7], (DEPTH, CONV_K, CONV_WIDTH), CONV_K ** -0.5),
        "w_br_gla": nrm(ks[8], (DEPTH, GLA_V, D), GLA_V ** -0.5),
        "w_br_conv": nrm(ks[9], (DEPTH, CONV_WIDTH, D), CONV_WIDTH ** -0.5),
        "w_mem_kv": nrm(ks[10], (DEPTH, D, 2 * XA_W), D ** -0.5),
        "w_br_xa": nrm(ks[11], (DEPTH, XA_W, D), XA_W ** -0.5),
        "b_gate": nrm(ks[12], (DEPTH, N_BRANCH * D), 0.1),
        "w_o": nrm(ks[13], (DEPTH, D, D), D ** -0.5),
        "mem_norm": gain(ks[14], (D,)),
        "norm_ffn": gain(ks[15], (DEPTH, D)),
        "peer_wq": nrm(ks[16], (DEPTH, D, PEER_HEADS * PEER_DQ), D ** -0.5),
        "peer_subkeys": nrm(ks[17], (DEPTH, PEER_HEADS, 2, PEER_NKEYS, PEER_DQ // 2), (PEER_DQ // 2) ** -0.5),
        "peer_u": nrm(ks[18], (DEPTH, PEER_EXPERTS, D), D ** -0.5),
        "peer_v": nrm(ks[19], (DEPTH, PEER_EXPERTS, D), (PEER_HEADS * PEER_TOPK) ** -0.5),
        "final_norm": gain(ks[20], (D,)),
    }


def reference(x, mem, norm_mix, w_in, w_a_up, b_a, gla_norm, conv_w, w_br_gla, w_br_conv,
              w_mem_kv, w_br_xa, b_gate, w_o, mem_norm, norm_ffn, peer_wq, peer_subkeys,
              peer_u, peer_v, final_norm):
    B, S, D = x.shape
    split_at = [int(c) for c in np.cumsum(SPLIT_SIZES)[:-1]]
    mem_n = rms_norm(mem, mem_norm)

    for l in range(DEPTH):
        hn = rms_norm(x, norm_mix[l])
        proj = hn @ w_in[l]
        q_g, k_g, v_g, r_g, a_lr, c_b, c_c, c_h, q_x, gates = jnp.split(proj, split_at, axis=-1)

        log_a = jax.nn.log_sigmoid((a_lr @ w_a_up[l] + b_a[l]).astype(jnp.float32)) / GLA_TAU
        o_g = gla_chunked(q_g.reshape(B, S, GLA_HEADS, GLA_DK),
                          k_g.reshape(B, S, GLA_HEADS, GLA_DK),
                          v_g.reshape(B, S, GLA_HEADS, GLA_DV),
                          log_a.reshape(B, S, GLA_HEADS, GLA_DK))
        o_g = rms_norm(o_g, gla_norm[l]).reshape(B, S, GLA_V).astype(x.dtype)
        y_gla = (o_g * jax.nn.silu(r_g)) @ w_br_gla[l]

        y_conv = (c_b * causal_short_conv(c_c * c_h, conv_w[l])) @ w_br_conv[l]

        mem_kv = mem_n @ w_mem_kv[l]
        y_xa = memory_cross_attention(q_x, mem_kv) @ w_br_xa[l]

        g = jax.nn.sigmoid((gates + b_gate[l]).astype(jnp.float32)).astype(x.dtype).reshape(B, S, N_BRANCH, D)
        merged = g[:, :, 0] * y_gla + g[:, :, 1] * y_conv + g[:, :, 2] * y_xa
        x = x + merged @ w_o[l]

        x = x + peer(rms_norm(x, norm_ffn[l]), peer_wq[l], peer_subkeys[l], peer_u[l], peer_v[l])

    return rms_norm(x, final_norm)
```

```python
import functools
import math

import jax
import jax.numpy as jnp
from jax import lax
from jax.experimental import pallas as pl
from jax.experimental.pallas import tpu as pltpu

F32 = jnp.float32
BF16 = jnp.bfloat16
EPS = 1e-6

GLA_HEADS = 4
GLA_CHUNK = 16
GLA_TAU = 16.0
XA_HEADS = 4
PEER_TOPK = 16

LANES = 128
SUBLANES = 8
VMEM_LIMIT_BYTES = 56 * 1024 * 1024

_NT = (((1,), (1,)), ((), ()))
_TN = (((0,), (0,)), ((), ()))


def _params(*sem):
    return pltpu.CompilerParams(dimension_semantics=sem, vmem_limit_bytes=VMEM_LIMIT_BYTES)


def _tile(n, pref):
    t = min(n, pref)
    while n % t:
        t -= 1
    return t


def _rmsnorm_kernel(x_ref, g_ref, o_ref):
    x = x_ref[...]
    ms = jnp.mean(x * x, axis=-1, keepdims=True)
    o_ref[...] = (x * lax.rsqrt(ms + EPS) * g_ref[...]).astype(o_ref.dtype)


def _add_rmsnorm_kernel(x_ref, y_ref, g_ref, o_ref):
    x = x_ref[...] + y_ref[...]
    ms = jnp.mean(x * x, axis=-1, keepdims=True)
    o_ref[...] = (x * lax.rsqrt(ms + EPS) * g_ref[...]).astype(o_ref.dtype)


def _rmsnorm(x, g, out_dtype, y=None):
    m, d = x.shape
    tm = _tile(m, 256)
    row = pl.BlockSpec((tm, d), lambda i: (i, 0))
    gain = pl.BlockSpec((1, d), lambda i: (0, 0))
    g2 = g.reshape(1, d).astype(F32)
    if y is None:
        body, specs, args = _rmsnorm_kernel, [row, gain], (x, g2)
    else:
        body, specs, args = _add_rmsnorm_kernel, [row, row, gain], (x, y, g2)
    return pl.pallas_call(
        body, grid=(m // tm,), in_specs=specs, out_specs=row,
        out_shape=jax.ShapeDtypeStruct((m, d), out_dtype),
        compiler_params=_params("parallel"))(*args)


def _mm_kernel(x_ref, w_ref, o_ref):
    o_ref[...] = jnp.dot(x_ref[...], w_ref[...], preferred_element_type=F32).astype(o_ref.dtype)


def _mm_res_kernel(x_ref, w_ref, r_ref, o_ref):
    acc = jnp.dot(x_ref[...], w_ref[...], preferred_element_type=F32)
    o_ref[...] = (r_ref[...] + acc).astype(o_ref.dtype)


def _mm(x, w, out_dtype, residual=None, tm=1024, tn=1024):
    m, k = x.shape
    n = w.shape[1]
    tm, tn = _tile(m, tm), _tile(n, tn)
    xs = pl.BlockSpec((tm, k), lambda i, j: (i, 0))
    ws = pl.BlockSpec((k, tn), lambda i, j: (0, j))
    os_ = pl.BlockSpec((tm, tn), lambda i, j: (i, j))
    if residual is None:
        body, specs, args = _mm_kernel, [xs, ws], (x, w)
    else:
        body, specs, args = _mm_res_kernel, [xs, ws, os_], (x, w, residual)
    return pl.pallas_call(
        body, grid=(m // tm, n // tn), in_specs=specs, out_specs=os_,
        out_shape=jax.ShapeDtypeStruct((m, n), out_dtype),
        compiler_params=_params("parallel", "parallel"))(*args)


def _gla_kernel(q_ref, k_ref, v_ref, r_ref, a_ref, wup_ref, ba_ref, gn_ref, o_ref, b_scr, s_scr,
                *, heads, chunk, tau):
    @pl.when(pl.program_id(1) == 0)
    def _():
        s_scr[...] = jnp.zeros_like(s_scr)

    blk = q_ref.shape[0]
    dk = q_ref.shape[1] // heads
    dv = v_ref.shape[1] // heads
    shift = chunk.bit_length() - 1

    z = jnp.dot(a_ref[...], wup_ref[...], preferred_element_type=F32) + ba_ref[...]
    log_a = -(jnp.maximum(-z, 0.0) + jnp.log1p(jnp.exp(-jnp.abs(z)))) * (1.0 / tau)
    row = lax.broadcasted_iota(jnp.int32, (blk, blk), 0)
    col = lax.broadcasted_iota(jnp.int32, (blk, blk), 1)
    same_chunk = jnp.right_shift(row, shift) == jnp.right_shift(col, shift)
    cum = jnp.where(same_chunk & (col <= row), 1.0, 0.0).astype(BF16)
    hi = log_a.astype(BF16)
    lo = (log_a - hi.astype(F32)).astype(BF16)
    b_scr[...] = (jnp.dot(cum, hi, preferred_element_type=F32)
                  + jnp.dot(cum, lo, preferred_element_type=F32))

    causal = (lax.broadcasted_iota(jnp.int32, (chunk, chunk), 1)
              <= lax.broadcasted_iota(jnp.int32, (chunk, chunk), 0))
    gn = gn_ref[...]

    def body(c, carry):
        rows = pl.ds(pl.multiple_of(c * chunk, chunk), chunk)
        for h in range(heads):
            kc = slice(h * dk, (h + 1) * dk)
            vc = slice(h * dv, (h + 1) * dv)
            b = b_scr[rows, kc]
            b_last = b[chunk - 1:chunk, :]
            q = q_ref[rows, kc].astype(F32)
            k = k_ref[rows, kc].astype(F32)
            v = v_ref[rows, vc]
            q_t = (q * jnp.exp(b) * (dk ** -0.5)).astype(BF16)
            k_in = (k * jnp.exp(-b)).astype(BF16)
            k_out = (k * jnp.exp(b_last - b)).astype(BF16)
            att = lax.dot_general(q_t, k_in, _NT, preferred_element_type=F32)
            att = jnp.where(causal, att, 0.0).astype(BF16)
            st = s_scr[h]
            o = (jnp.dot(att, v, preferred_element_type=F32)
                 + lax.dot_general(q_t, st.astype(BF16), _NT, preferred_element_type=F32))
            s_scr[h] = st * jnp.exp(b_last) + lax.dot_general(
                v, k_out, _TN, preferred_element_type=F32)
            ms = jnp.mean(o * o, axis=-1, keepdims=True)
            y = o * lax.rsqrt(ms + EPS) * gn
            r = r_ref[rows, vc].astype(F32)
            o_ref[rows, vc] = (y * (r * jax.nn.sigmoid(r))).astype(o_ref.dtype)
        return carry

    lax.fori_loop(0, blk // chunk, body, 0)


def _gla(proj, a_lr, wup, ba, gn, *, batch, seq, qk, vw, cols):
    blk = _tile(seq, 256)
    nb = seq // blk
    dv = vw // GLA_HEADS

    def at(width, off):
        return pl.BlockSpec((blk, width), lambda b, j: (b * nb + j, off // width))

    full = lambda shape: pl.BlockSpec(shape, lambda b, j: (0,) * len(shape))
    return pl.pallas_call(
        functools.partial(_gla_kernel, heads=GLA_HEADS, chunk=GLA_CHUNK, tau=GLA_TAU),
        grid=(batch, nb),
        in_specs=[at(qk, cols["q"]), at(qk, cols["k"]), at(vw, cols["v"]), at(vw, cols["r"]),
                  pl.BlockSpec((blk, a_lr.shape[1]), lambda b, j: (b * nb + j, 0)),
                  full(wup.shape), full(ba.shape), full(gn.shape)],
        out_specs=pl.BlockSpec((blk, vw), lambda b, j: (b * nb + j, 0)),
        out_shape=jax.ShapeDtypeStruct((batch * seq, vw), BF16),
        scratch_shapes=[pltpu.VMEM((blk, qk), F32),
                        pltpu.VMEM((GLA_HEADS, dv, qk // GLA_HEADS), F32)],
        compiler_params=_params("parallel", "arbitrary"),
    )(proj, proj, proj, proj, a_lr, wup, ba, gn)


def _conv_kernel(cb_ref, cc_ref, ch_ref, pc_ref, ph_ref, w_ref, o_ref, u_scr):
    ts = cc_ref.shape[0]
    halo = pc_ref.shape[0]
    u = cc_ref[...].astype(F32) * ch_ref[...].astype(F32)
    prev = pc_ref[...].astype(F32) * ph_ref[...].astype(F32)
    u_scr[0:halo, :] = jnp.where(pl.program_id(1) > 0, prev, 0.0)
    u_scr[halo:halo + ts, :] = u
    w = w_ref[...]
    y = (w[0:1] * u + w[1:2] * u_scr[halo - 1:halo - 1 + ts, :]
         + w[2:3] * u_scr[halo - 2:halo - 2 + ts, :])
    o_ref[...] = (cb_ref[...].astype(F32) * y).astype(o_ref.dtype)


def _conv(proj, conv_w, *, batch, seq, width, cols):
    ts = _tile(seq, 512)
    tw = _tile(width, 1024)
    halo = 16
    ns, nw = seq // ts, width // tw

    def at(off):
        return pl.BlockSpec((ts, tw), lambda b, j, c: (b * ns + j, off // tw + c))

    def halo_at(off):
        return pl.BlockSpec(
            (halo, tw),
            lambda b, j, c: (jnp.maximum((b * seq + j * ts) // halo - 1, 0), off // tw + c))

    return pl.pallas_call(
        _conv_kernel, grid=(batch, ns, nw),
        in_specs=[at(cols["cb"]), at(cols["cc"]), at(cols["ch"]),
                  halo_at(cols["cc"]), halo_at(cols["ch"]),
                  pl.BlockSpec((conv_w.shape[0], tw), lambda b, j, c: (0, c))],
        out_specs=pl.BlockSpec((ts, tw), lambda b, j, c: (b * ns + j, c)),
        out_shape=jax.ShapeDtypeStruct((batch * seq, width), BF16),
        scratch_shapes=[pltpu.VMEM((halo + ts, tw), F32)],
        compiler_params=_params("parallel", "parallel", "parallel"),
    )(proj, proj, proj, proj, proj, conv_w)


def _xa_kernel(q_ref, k_ref, v_ref, o_ref, *, heads):
    dh = q_ref.shape[1] // heads
    for h in range(heads):
        cs = slice(h * dh, (h + 1) * dh)
        s = lax.dot_general(q_ref[:, cs], k_ref[:, cs], _NT,
                            preferred_element_type=F32) * (dh ** -0.5)
        p = jnp.exp(s - jnp.max(s, axis=-1, keepdims=True))
        l = jnp.sum(p, axis=-1, keepdims=True)
        o = jnp.dot(p.astype(BF16), v_ref[:, cs], preferred_element_type=F32)
        o_ref[:, cs] = (o / l).astype(o_ref.dtype)


def _xattn(proj, mem_kv, *, batch, seq, n_mem, width, col):
    tq = _tile(seq, 512)
    nq = seq // tq
    return pl.pallas_call(
        functools.partial(_xa_kernel, heads=XA_HEADS), grid=(batch, nq),
        in_specs=[pl.BlockSpec((tq, width), lambda b, j: (b * nq + j, col // width)),
                  pl.BlockSpec((n_mem, width), lambda b, j: (b, 0)),
                  pl.BlockSpec((n_mem, width), lambda b, j: (b, 1))],
        out_specs=pl.BlockSpec((tq, width), lambda b, j: (b * nq + j, 0)),
        out_shape=jax.ShapeDtypeStruct((batch * seq, width), BF16),
        compiler_params=_params("parallel", "parallel"),
    )(proj, mem_kv, mem_kv)


def _merge_kernel(og_ref, oc_ref, ox_ref, wg_ref, wc_ref, wx_ref, g0_ref, g1_ref, g2_ref,
                  bg_ref, o_ref):
    acc = None
    for idx, (a_ref, w_ref, g_ref) in enumerate(
            ((og_ref, wg_ref, g0_ref), (oc_ref, wc_ref, g1_ref), (ox_ref, wx_ref, g2_ref))):
        y = jnp.dot(a_ref[...], w_ref[...], preferred_element_type=F32)
        g = jax.nn.sigmoid(g_ref[...].astype(F32) + bg_ref[idx:idx + 1, :])
        acc = g * y if acc is None else acc + g * y
    o_ref[...] = acc.astype(o_ref.dtype)


def _merge(o_gla, o_conv, o_xa, w_gla, w_conv, w_xa, proj, b_gate, *, d, gate_col):
    m = o_gla.shape[0]
    tm, tn = _tile(m, 512), _tile(d, 1024)

    def lhs(a):
        return pl.BlockSpec((tm, a.shape[1]), lambda i, j: (i, 0))

    def rhs(w):
        return pl.BlockSpec((w.shape[0], tn), lambda i, j: (0, j))

    def gate(g):
        return pl.BlockSpec((tm, tn), lambda i, j: (i, (gate_col + g * d) // tn + j))

    return pl.pallas_call(
        _merge_kernel, grid=(m // tm, d // tn),
        in_specs=[lhs(o_gla), lhs(o_conv), lhs(o_xa), rhs(w_gla), rhs(w_conv), rhs(w_xa),
                  gate(0), gate(1), gate(2),
                  pl.BlockSpec((b_gate.shape[0], tn), lambda i, j: (0, j))],
        out_specs=pl.BlockSpec((tm, tn), lambda i, j: (i, j)),
        out_shape=jax.ShapeDtypeStruct((m, d), BF16),
        compiler_params=_params("parallel", "parallel"),
    )(o_gla, o_conv, o_xa, w_gla, w_conv, w_xa, proj, proj, proj, b_gate)


def _n_candidates(topk):
    return sum(topk // p for p in range(1, topk + 1))


def _route_kernel(q_ref, sk_ref, e1_ref, e2_ref, pth_ref, a_scr, b_scr, c_scr, *, topk):
    dq = sk_ref.shape[3]

    def half(c, top_scr):
        s = lax.dot_general(sk_ref[0, c], q_ref[:, c * dq:(c + 1) * dq], _NT,
                            preferred_element_type=F32)
        e = jnp.exp(s - jnp.max(s, axis=0, keepdims=True))
        work = e
        for r in range(topk):
            top = jnp.max(work, axis=0, keepdims=True)
            top_scr[r:r + 1, :] = jnp.maximum(top, 0.0)
            work = jnp.where(work == top, -1.0, work)
        return e

    e1 = half(0, a_scr)
    e2 = half(1, b_scr)

    def candidates(scale):
        off = 0
        for p in range(topk):
            nq = topk // (p + 1)
            c_scr[off:off + nq, :] = a_scr[p:p + 1, :] * (b_scr[0:nq, :] * scale)
            off += nq
        if off < c_scr.shape[0]:
            c_scr[off:, :] = jnp.full((c_scr.shape[0] - off, c_scr.shape[1]), -1.0, F32)
        return c_scr[...]

    cand = candidates(1.0)
    work = cand
    kth = None
    for r in range(topk):
        kth = jnp.max(work, axis=0, keepdims=True)
        work = jnp.where(work == kth, -1.0, work)
    kth = jnp.maximum(kth, 0.0)
    sel = cand >= kth
    inv_z = 1.0 / jnp.sum(jnp.where(sel, cand, 0.0), axis=0, keepdims=True)
    cand_n = candidates(inv_z)
    e1_ref[0] = e1
    e2_ref[0] = e2 * inv_z
    pth_ref[0] = jnp.min(jnp.where(sel, cand_n, jnp.inf), axis=0, keepdims=True)


def _route(q2, sub_keys):
    t = q2.shape[0]
    heads, _, nkeys, dq = sub_keys.shape
    tm = _tile(t, 256)
    ncand = -(-_n_candidates(PEER_TOPK) // 8) * 8
    fac = pl.BlockSpec((1, nkeys, tm), lambda i, h: (h, 0, i))
    return pl.pallas_call(
        functools.partial(_route_kernel, topk=PEER_TOPK), grid=(t // tm, heads),
        in_specs=[pl.BlockSpec((tm, 2 * dq), lambda i, h: (i, h)),
                  pl.BlockSpec((1, 2, nkeys, dq), lambda i, h: (h, 0, 0, 0))],
        out_specs=[fac, fac, pl.BlockSpec((1, 1, tm), lambda i, h: (h, 0, i))],
        out_shape=[jax.ShapeDtypeStruct((heads, nkeys, t), F32),
                   jax.ShapeDtypeStruct((heads, nkeys, t), F32),
                   jax.ShapeDtypeStruct((heads, 1, t), F32)],
        scratch_shapes=[pltpu.VMEM((PEER_TOPK, tm), F32), pltpu.VMEM((PEER_TOPK, tm), F32),
                        pltpu.VMEM((ncand, tm), F32)],
        compiler_params=_params("parallel", "parallel"),
    )(q2, sub_keys)


def _peer_kernel(h_ref, u_ref, v_ref, e1_ref, e2_ref, pth_ref, o_ref, w_scr, *, heads, nkeys):
    kstep = pl.program_id(1)

    @pl.when(kstep == 0)
    def _():
        o_ref[...] = jnp.zeros_like(o_ref)

    tm = h_ref.shape[0]
    te = u_ref.shape[0]
    ni = te // nkeys
    a_t = lax.dot_general(u_ref[...], h_ref[...], _NT, preferred_element_type=F32)
    parts = SUBLANES // ni
    group = pl.ds(pl.multiple_of((kstep // parts) * SUBLANES, SUBLANES), SUBLANES)
    part = kstep % parts
    for tc in range(tm // LANES):
        ls = slice(tc * LANES, (tc + 1) * LANES)
        e1_rows = []
        for h in range(heads):
            grp = e1_ref[h, group, ls]
            rows = grp[0:ni]
            for s in range(1, parts):
                rows = jnp.where(part == s, grp[s * ni:(s + 1) * ni], rows)
            e1_rows.append(rows)
        for ii in range(ni):
            rs = slice(ii * nkeys, (ii + 1) * nkeys)
            g = jnp.zeros((nkeys, LANES), F32)
            for h in range(heads):
                p = e1_rows[h][ii:ii + 1, :] * e2_ref[h, :, ls]
                g = g + jnp.where(p >= pth_ref[h, :, ls], p, 0.0)
            a = a_t[rs, ls]
            gelu = 0.5 * a * (1.0 + lax.erf(a * (1.0 / math.sqrt(2.0))))
            w_scr[rs, ls] = (g * gelu).astype(BF16)
    o_ref[...] += lax.dot_general(w_scr[...], v_ref[...], _TN, preferred_element_type=F32)


def _peer_mix(h2, u, v, e1, e2, pth):
    t, d = h2.shape
    n_exp = u.shape[0]
    heads, nkeys, _ = e1.shape
    tm, te = _tile(t, 512), _tile(n_exp, 512)
    fac = pl.BlockSpec((heads, nkeys, tm), lambda i, k: (0, 0, i))
    return pl.pallas_call(
        functools.partial(_peer_kernel, heads=heads, nkeys=nkeys),
        grid=(t // tm, n_exp // te),
        in_specs=[pl.BlockSpec((tm, d), lambda i, k: (i, 0)),
                  pl.BlockSpec((te, d), lambda i, k: (k, 0)),
                  pl.BlockSpec((te, d), lambda i, k: (k, 0)),
                  fac, fac, pl.BlockSpec((heads, 1, tm), lambda i, k: (0, 0, i))],
        out_specs=pl.BlockSpec((tm, d), lambda i, k: (i, 0)),
        out_shape=jax.ShapeDtypeStruct((t, d), F32),
        scratch_shapes=[pltpu.VMEM((te, tm), BF16)],
        compiler_params=_params("parallel", "arbitrary"),
    )(h2, u, v, e1, e2, pth)


def kernel(x, mem, norm_mix, w_in, w_a_up, b_a, gla_norm, conv_w, w_br_gla, w_br_conv, w_mem_kv,
           w_br_xa, b_gate, w_o, mem_norm, norm_ffn, peer_wq, peer_subkeys, peer_u, peer_v,
           final_norm):
    batch, seq, d = x.shape
    n_mem = mem.shape[1]
    depth = w_in.shape[0]
    rank, qk = w_a_up.shape[1], w_a_up.shape[2]
    vw = w_br_gla.shape[1]
    cw = w_br_conv.shape[1]
    xw = w_br_xa.shape[1]
    t = batch * seq

    a_off = 2 * qk + 2 * vw
    cols = {"q": 0, "k": qk, "v": 2 * qk, "r": 2 * qk + vw,
            "cb": a_off, "cc": a_off + cw, "ch": a_off + 2 * cw}
    xq_col = a_off + 3 * cw
    gate_col = xq_col + xw
    assert rank <= LANES and gate_col + 3 * d + rank == w_in.shape[2]

    xs = x.reshape(t, d)
    mem_n = _rmsnorm(mem.reshape(batch * n_mem, d), mem_norm, BF16)

    for l in range(depth):
        w_main = jnp.concatenate([w_in[l, :, :a_off], w_in[l, :, a_off + rank:]], axis=1).astype(BF16)
        w_a = jnp.pad(w_in[l, :, a_off:a_off + rank], ((0, 0), (0, LANES - rank))).astype(BF16)
        wup = jnp.pad(w_a_up[l], ((0, LANES - rank), (0, 0))).astype(BF16)

        hn = _rmsnorm(xs, norm_mix[l], BF16)
        proj = _mm(hn, w_main, BF16)
        a_lr = _mm(hn, w_a, BF16)

        o_gla = _gla(proj, a_lr, wup, b_a[l].reshape(1, qk), gla_norm[l].reshape(1, -1),
                     batch=batch, seq=seq, qk=qk, vw=vw, cols=cols)
        o_conv = _conv(proj, conv_w[l], batch=batch, seq=seq, width=cw, cols=cols)
        mem_kv = _mm(mem_n, w_mem_kv[l].astype(BF16), BF16)
        o_xa = _xattn(proj, mem_kv, batch=batch, seq=seq, n_mem=n_mem, width=xw, col=xq_col)

        merged = _merge(o_gla, o_conv, o_xa, w_br_gla[l].astype(BF16), w_br_conv[l].astype(BF16),
                        w_br_xa[l].astype(BF16), proj, b_gate[l].reshape(3, d), d=d,
                        gate_col=gate_col)
        xs = _mm(merged, w_o[l].astype(BF16), F32, residual=xs)

        h2 = _rmsnorm(xs, norm_ffn[l], BF16)
        q2 = _mm(h2, peer_wq[l].astype(BF16), BF16)
        e1, e2, pth = _route(q2, peer_subkeys[l].astype(BF16))
        y = _peer_mix(h2, peer_u[l].astype(BF16), peer_v[l].astype(BF16), e1, e2, pth)
        if l + 1 < depth:
            xs = xs + y
    return _rmsnorm(xs, final_norm, F32, y=y).reshape(batch, seq, d)
```

```python
import functools
import math

import jax
import jax.numpy as jnp
from jax import lax
from jax.experimental import pallas as pl
from jax.experimental.pallas import tpu as pltpu

F32 = jnp.float32
BF16 = jnp.bfloat16
EPS = 1e-6

GLA_HEADS = 4
GLA_CHUNK = 16
GLA_TAU = 16.0
XA_HEADS = 4
PEER_TOPK = 16

LANES = 128
SUBLANES = 8
VMEM_LIMIT_BYTES = 56 * 1024 * 1024

_NT = (((1,), (1,)), ((), ()))
_TN = (((0,), (0,)), ((), ()))


def _params(*sem):
    return pltpu.CompilerParams(dimension_semantics=sem, vmem_limit_bytes=VMEM_LIMIT_BYTES)


def _tile(n, pref):
    t = min(n, pref)
    while n % t:
        t -= 1
    return t


def _rmsnorm_kernel(x_ref, g_ref, o_ref):
    x = x_ref[...]
    ms = jnp.mean(x * x, axis=-1, keepdims=True)
    o_ref[...] = (x * lax.rsqrt(ms + EPS) * g_ref[...]).astype(o_ref.dtype)


def _add_rmsnorm_kernel(x_ref, y_ref, g_ref, o_ref):
    x = x_ref[...] + y_ref[...]
    ms = jnp.mean(x * x, axis=-1, keepdims=True)
    o_ref[...] = (x * lax.rsqrt(ms + EPS) * g_ref[...]).astype(o_ref.dtype)


def _rmsnorm(x, g, out_dtype, y=None):
    m, d = x.shape
    tm = _tile(m, 256)
    row = pl.BlockSpec((tm, d), lambda i: (i, 0))
    gain = pl.BlockSpec((1, d), lambda i: (0, 0))
    g2 = g.reshape(1, d).astype(F32)
    if y is None:
        body, specs, args = _rmsnorm_kernel, [row, gain], (x, g2)
    else:
        body, specs, args = _add_rmsnorm_kernel, [row, row, gain], (x, y, g2)
    return pl.pallas_call(
        body, grid=(m // tm,), in_specs=specs, out_specs=row,
        out_shape=jax.ShapeDtypeStruct((m, d), out_dtype),
        compiler_params=_params("parallel"))(*args)


def _mm_kernel(x_ref, w_ref, o_ref):
    o_ref[...] = jnp.dot(x_ref[...], w_ref[...], preferred_element_type=F32).astype(o_ref.dtype)


def _mm_res_kernel(x_ref, w_ref, r_ref, o_ref):
    acc = jnp.dot(x_ref[...], w_ref[...], preferred_element_type=F32)
    o_ref[...] = (r_ref[...] + acc).astype(o_ref.dtype)


def _mm(x, w, out_dtype, residual=None, tm=1024, tn=1024):
    m, k = x.shape
    n = w.shape[1]
    tm, tn = _tile(m, tm), _tile(n, tn)
    xs = pl.BlockSpec((tm, k), lambda i, j: (i, 0))
    ws = pl.BlockSpec((k, tn), lambda i, j: (0, j))
    os_ = pl.BlockSpec((tm, tn), lambda i, j: (i, j))
    if residual is None:
        body, specs, args = _mm_kernel, [xs, ws], (x, w)
    else:
        body, specs, args = _mm_res_kernel, [xs, ws, os_], (x, w, residual)
    return pl.pallas_call(
        body, grid=(m // tm, n // tn), in_specs=specs, out_specs=os_,
        out_shape=jax.ShapeDtypeStruct((m, n), out_dtype),
        compiler_params=_params("parallel", "parallel"))(*args)


def _cast_shift_kernel(a_ref, b_ref, o_ref, *, shift):
    tn = a_ref.shape[1]
    lane = lax.broadcasted_iota(jnp.int32, a_ref.shape, 1)
    a = pltpu.roll(a_ref[...], tn - shift, axis=1)
    b = pltpu.roll(b_ref[...], tn - shift, axis=1)
    o_ref[...] = jnp.where(lane < tn - shift, a, b).astype(o_ref.dtype)


def _cast_kernel(a_ref, o_ref):
    o_ref[...] = a_ref[...].astype(o_ref.dtype)


def _cast_cols(w, start, width):
    k = w.shape[0]
    tr, tn = _tile(k, 512), _tile(width, 512)
    shift = start % tn
    first = start // tn
    specs = [pl.BlockSpec((tr, tn), lambda i, j: (i, first + j))]
    body = _cast_kernel
    if shift:
        specs.append(pl.BlockSpec((tr, tn), lambda i, j: (i, first + j + 1)))
        body = functools.partial(_cast_shift_kernel, shift=shift)
    return pl.pallas_call(
        body, grid=(k // tr, width // tn), in_specs=specs,
        out_specs=pl.BlockSpec((tr, tn), lambda i, j: (i, j)),
        out_shape=jax.ShapeDtypeStruct((k, width), BF16),
        compiler_params=_params("parallel", "parallel"))(*([w] * len(specs)))


def _gla_kernel(q_ref, k_ref, v_ref, r_ref, a_ref, wup_ref, ba_ref, gn_ref, o_ref, b_scr, s_scr,
                *, heads, chunk, tau):
    @pl.when(pl.program_id(1) == 0)
    def _():
        s_scr[...] = jnp.zeros_like(s_scr)

    blk = q_ref.shape[0]
    dk = q_ref.shape[1] // heads
    dv = v_ref.shape[1] // heads
    shift = chunk.bit_length() - 1

    z = jnp.dot(a_ref[...], wup_ref[...], preferred_element_type=F32) + ba_ref[...]
    log_a = -(jnp.maximum(-z, 0.0) + jnp.log1p(jnp.exp(-jnp.abs(z)))) * (1.0 / tau)
    row = lax.broadcasted_iota(jnp.int32, (blk, blk), 0)
    col = lax.broadcasted_iota(jnp.int32, (blk, blk), 1)
    same_chunk = jnp.right_shift(row, shift) == jnp.right_shift(col, shift)
    cum = jnp.where(same_chunk & (col <= row), 1.0, 0.0).astype(BF16)
    hi = log_a.astype(BF16)
    lo = (log_a - hi.astype(F32)).astype(BF16)
    b_scr[...] = (jnp.dot(cum, hi, preferred_element_type=F32)
                  + jnp.dot(cum, lo, preferred_element_type=F32))

    causal = (lax.broadcasted_iota(jnp.int32, (chunk, chunk), 1)
              <= lax.broadcasted_iota(jnp.int32, (chunk, chunk), 0))
    gn = gn_ref[...]

    def body(c, carry):
        rows = pl.ds(pl.multiple_of(c * chunk, chunk), chunk)
        for h in range(heads):
            kc = slice(h * dk, (h + 1) * dk)
            vc = slice(h * dv, (h + 1) * dv)
            b = b_scr[rows, kc]
            b_last = b[chunk - 1:chunk, :]
            q = q_ref[rows, kc].astype(F32)
            k = k_ref[rows, kc].astype(F32)
            v = v_ref[rows, vc]
            q_t = (q * jnp.exp(b) * (dk ** -0.5)).astype(BF16)
            k_in = (k * jnp.exp(-b)).astype(BF16)
            k_out = (k * jnp.exp(b_last - b)).astype(BF16)
            att = lax.dot_general(q_t, k_in, _NT, preferred_element_type=F32)
            att = jnp.where(causal, att, 0.0).astype(BF16)
            st = s_scr[h]
            o = (jnp.dot(att, v, preferred_element_type=F32)
                 + lax.dot_general(q_t, st.astype(BF16), _NT, preferred_element_type=F32))
            s_scr[h] = st * jnp.exp(b_last) + lax.dot_general(
                v, k_out, _TN, preferred_element_type=F32)
            ms = jnp.mean(o * o, axis=-1, keepdims=True)
            y = o * lax.rsqrt(ms + EPS) * gn
            r = r_ref[rows, vc].astype(F32)
            o_ref[rows, vc] = (y * (r * jax.nn.sigmoid(r))).astype(o_ref.dtype)
        return carry

    lax.fori_loop(0, blk // chunk, body, 0)


def _gla(proj, a_lr, wup, ba, gn, *, batch, seq, qk, vw, cols):
    blk = _tile(seq, 256)
    nb = seq // blk
    dv = vw // GLA_HEADS

    def at(width, off):
        return pl.BlockSpec((blk, width), lambda b, j: (b * nb + j, off // width))

    full = lambda shape: pl.BlockSpec(shape, lambda b, j: (0,) * len(shape))
    return pl.pallas_call(
        functools.partial(_gla_kernel, heads=GLA_HEADS, chunk=GLA_CHUNK, tau=GLA_TAU),
        grid=(batch, nb),
        in_specs=[at(qk, cols["q"]), at(qk, cols["k"]), at(vw, cols["v"]), at(vw, cols["r"]),
                  pl.BlockSpec((blk, a_lr.shape[1]), lambda b, j: (b * nb + j, 0)),
                  full(wup.shape), full(ba.shape), full(gn.shape)],
        out_specs=pl.BlockSpec((blk, vw), lambda b, j: (b * nb + j, 0)),
        out_shape=jax.ShapeDtypeStruct((batch * seq, vw), BF16),
        scratch_shapes=[pltpu.VMEM((blk, qk), F32),
                        pltpu.VMEM((GLA_HEADS, dv, qk // GLA_HEADS), F32)],
        compiler_params=_params("parallel", "arbitrary"),
    )(proj, proj, proj, proj, a_lr, wup, ba, gn)


def _conv_kernel(cb_ref, cc_ref, ch_ref, pc_ref, ph_ref, w_ref, o_ref, u_scr):
    ts = cc_ref.shape[0]
    halo = pc_ref.shape[0]
    u = cc_ref[...].astype(F32) * ch_ref[...].astype(F32)
    prev = pc_ref[...].astype(F32) * ph_ref[...].astype(F32)
    u_scr[0:halo, :] = jnp.where(pl.program_id(1) > 0, prev, 0.0)
    u_scr[halo:halo + ts, :] = u
    w = w_ref[...]
    y = (w[0:1] * u + w[1:2] * u_scr[halo - 1:halo - 1 + ts, :]
         + w[2:3] * u_scr[halo - 2:halo - 2 + ts, :])
    o_ref[...] = (cb_ref[...].astype(F32) * y).astype(o_ref.dtype)


def _conv(proj, conv_w, *, batch, seq, width, cols):
    ts = _tile(seq, 512)
    tw = _tile(width, 1024)
    halo = 16
    ns, nw = seq // ts, width // tw

    def at(off):
        return pl.BlockSpec((ts, tw), lambda b, j, c: (b * ns + j, off // tw + c))

    def halo_at(off):
        return pl.BlockSpec(
            (halo, tw),
            lambda b, j, c: (jnp.maximum((b * seq + j * ts) // halo - 1, 0), off // tw + c))

    return pl.pallas_call(
        _conv_kernel, grid=(batch, ns, nw),
        in_specs=[at(cols["cb"]), at(cols["cc"]), at(cols["ch"]),
                  halo_at(cols["cc"]), halo_at(cols["ch"]),
                  pl.BlockSpec((conv_w.shape[0], tw), lambda b, j, c: (0, c))],
        out_specs=pl.BlockSpec((ts, tw), lambda b, j, c: (b * ns + j, c)),
        out_shape=jax.ShapeDtypeStruct((batch * seq, width), BF16),
        scratch_shapes=[pltpu.VMEM((halo + ts, tw), F32)],
        compiler_params=_params("parallel", "parallel", "parallel"),
    )(proj, proj, proj, proj, proj, conv_w)


def _xa_kernel(q_ref, k_ref, v_ref, o_ref, *, heads):
    dh = q_ref.shape[1] // heads
    for h in range(heads):
        cs = slice(h * dh, (h + 1) * dh)
        s = lax.dot_general(q_ref[:, cs], k_ref[:, cs], _NT,
                            preferred_element_type=F32) * (dh ** -0.5)
        p = jnp.exp(s - jnp.max(s, axis=-1, keepdims=True))
        l = jnp.sum(p, axis=-1, keepdims=True)
        o = jnp.dot(p.astype(BF16), v_ref[:, cs], preferred_element_type=F32)
        o_ref[:, cs] = (o / l).astype(o_ref.dtype)


def _xattn(proj, mem_kv, *, batch, seq, n_mem, width, col):
    tq = _tile(seq, 512)
    nq = seq // tq
    return pl.pallas_call(
        functools.partial(_xa_kernel, heads=XA_HEADS), grid=(batch, nq),
        in_specs=[pl.BlockSpec((tq, width), lambda b, j: (b * nq + j, col // width)),
                  pl.BlockSpec((n_mem, width), lambda b, j: (b, 0)),
                  pl.BlockSpec((n_mem, width), lambda b, j: (b, 1))],
        out_specs=pl.BlockSpec((tq, width), lambda b, j: (b * nq + j, 0)),
        out_shape=jax.ShapeDtypeStruct((batch * seq, width), BF16),
        compiler_params=_params("parallel", "parallel"),
    )(proj, mem_kv, mem_kv)


def _merge_kernel(og_ref, oc_ref, ox_ref, wg_ref, wc_ref, wx_ref, g0_ref, g1_ref, g2_ref,
                  bg_ref, o_ref):
    acc = None
    for idx, (a_ref, w_ref, g_ref) in enumerate(
            ((og_ref, wg_ref, g0_ref), (oc_ref, wc_ref, g1_ref), (ox_ref, wx_ref, g2_ref))):
        y = jnp.dot(a_ref[...], w_ref[...], preferred_element_type=F32)
        g = jax.nn.sigmoid(g_ref[...].astype(F32) + bg_ref[idx:idx + 1, :])
        acc = g * y if acc is None else acc + g * y
    o_ref[...] = acc.astype(o_ref.dtype)


def _merge(o_gla, o_conv, o_xa, w_gla, w_conv, w_xa, proj, b_gate, *, d, gate_col):
    m = o_gla.shape[0]
    tm, tn = _tile(m, 512), _tile(d, 1024)

    def lhs(a):
        return pl.BlockSpec((tm, a.shape[1]), lambda i, j: (i, 0))

    def rhs(w):
        return pl.BlockSpec((w.shape[0], tn), lambda i, j: (0, j))

    def gate(g):
        return pl.BlockSpec((tm, tn), lambda i, j: (i, (gate_col + g * d) // tn + j))

    return pl.pallas_call(
        _merge_kernel, grid=(m // tm, d // tn),
        in_specs=[lhs(o_gla), lhs(o_conv), lhs(o_xa), rhs(w_gla), rhs(w_conv), rhs(w_xa),
                  gate(0), gate(1), gate(2),
                  pl.BlockSpec((b_gate.shape[0], tn), lambda i, j: (0, j))],
        out_specs=pl.BlockSpec((tm, tn), lambda i, j: (i, j)),
        out_shape=jax.ShapeDtypeStruct((m, d), BF16),
        compiler_params=_params("parallel", "parallel"),
    )(o_gla, o_conv, o_xa, w_gla, w_conv, w_xa, proj, proj, proj, b_gate)


def _n_candidates(topk):
    return sum(topk // p for p in range(1, topk + 1))


def _route_kernel(q_ref, sk_ref, e1_ref, e2_ref, pth_ref, a_scr, b_scr, c_scr, *, topk):
    dq = sk_ref.shape[3]

    def half(c, top_scr):
        s = lax.dot_general(sk_ref[0, c], q_ref[:, c * dq:(c + 1) * dq], _NT,
                            preferred_element_type=F32)
        e = jnp.exp(s - jnp.max(s, axis=0, keepdims=True))
        work = e
        for r in range(topk):
            top = jnp.max(work, axis=0, keepdims=True)
            top_scr[r:r + 1, :] = jnp.maximum(top, 0.0)
            work = jnp.where(work == top, -1.0, work)
        return e

    e1 = half(0, a_scr)
    e2 = half(1, b_scr)

    def candidates(scale):
        off = 0
        for p in range(topk):
            nq = topk // (p + 1)
            c_scr[off:off + nq, :] = a_scr[p:p + 1, :] * (b_scr[0:nq, :] * scale)
            off += nq
        if off < c_scr.shape[0]:
            c_scr[off:, :] = jnp.full((c_scr.shape[0] - off, c_scr.shape[1]), -1.0, F32)
        return c_scr[...]

    cand = candidates(1.0)
    work = cand
    kth = None
    for r in range(topk):
        kth = jnp.max(work, axis=0, keepdims=True)
        work = jnp.where(work == kth, -1.0, work)
    kth = jnp.maximum(kth, 0.0)
    sel = cand >= kth
    inv_z = 1.0 / jnp.sum(jnp.where(sel, cand, 0.0), axis=0, keepdims=True)
    cand_n = candidates(inv_z)
    e1_ref[0] = e1
    e2_ref[0] = e2 * inv_z
    pth_ref[0] = jnp.min(jnp.where(sel, cand_n, jnp.inf), axis=0, keepdims=True)


def _route(q2, sub_keys):
    t = q2.shape[0]
    heads, _, nkeys, dq = sub_keys.shape
    tm = _tile(t, 256)
    ncand = -(-_n_candidates(PEER_TOPK) // 8) * 8
    fac = pl.BlockSpec((1, nkeys, tm), lambda i, h: (h, 0, i))
    return pl.pallas_call(
        functools.partial(_route_kernel, topk=PEER_TOPK), grid=(t // tm, heads),
        in_specs=[pl.BlockSpec((tm, 2 * dq), lambda i, h: (i, h)),
                  pl.BlockSpec((1, 2, nkeys, dq), lambda i, h: (h, 0, 0, 0))],
        out_specs=[fac, fac, pl.BlockSpec((1, 1, tm), lambda i, h: (h, 0, i))],
        out_shape=[jax.ShapeDtypeStruct((heads, nkeys, t), F32),
                   jax.ShapeDtypeStruct((heads, nkeys, t), F32),
                   jax.ShapeDtypeStruct((heads, 1, t), F32)],
        scratch_shapes=[pltpu.VMEM((PEER_TOPK, tm), F32), pltpu.VMEM((PEER_TOPK, tm), F32),
                        pltpu.VMEM((ncand, tm), F32)],
        compiler_params=_params("parallel", "parallel"),
    )(q2, sub_keys)


def _peer_kernel(h_ref, u_ref, v_ref, e1_ref, e2_ref, pth_ref, o_ref, w_scr, a0_scr, a1_scr,
                 *, heads, nkeys):
    kstep = pl.program_id(1)
    tm = h_ref.shape[0]
    te = u_ref.shape[0]
    ni = te // nkeys

    @pl.when(kstep == 0)
    def _():
        o_ref[...] = jnp.zeros_like(o_ref)
        a1_scr[...] = jnp.zeros_like(a1_scr)

    kmix = jnp.maximum(kstep - 1, 0)
    parts = SUBLANES // ni
    group = pl.ds(pl.multiple_of((kmix // parts) * SUBLANES, SUBLANES), SUBLANES)
    part = kmix % parts

    def e1_row(h, ii, ls):
        grp = e1_ref[h, group, ls]
        row = grp[ii:ii + 1]
        for s in range(1, parts):
            row = jnp.where(part == s, grp[s * ni + ii:s * ni + ii + 1], row)
        return row

    def step(a_new, a_old):
        a_new[...] = lax.dot_general(u_ref[...], h_ref[...], _NT, preferred_element_type=F32)
        for ii in range(ni):
            rs = slice(ii * nkeys, (ii + 1) * nkeys)
            for tc in range(tm // LANES):
                ls = slice(tc * LANES, (tc + 1) * LANES)
                g = jnp.zeros((nkeys, LANES), F32)
                for h in range(heads):
                    p = e1_row(h, ii, ls) * e2_ref[h, :, ls]
                    g = g + jnp.where(p >= pth_ref[h, :, ls], p, 0.0)
                a = a_old[rs, ls]
                gelu = 0.5 * a * (1.0 + lax.erf(a * (1.0 / math.sqrt(2.0))))
                w_scr[rs, ls] = (g * gelu).astype(BF16)
        o_ref[...] += lax.dot_general(w_scr[...], v_ref[...], _TN, preferred_element_type=F32)

    parity = kstep % 2
    pl.when(parity == 0)(lambda: step(a0_scr, a1_scr))
    pl.when(parity == 1)(lambda: step(a1_scr, a0_scr))


def _peer_mix(h2, u, v, e1, e2, pth):
    t, d = h2.shape
    n_exp = u.shape[0]
    heads, nkeys, _ = e1.shape
    tm, te = _tile(t, 512), _tile(n_exp, 512)
    nk = n_exp // te
    fac = pl.BlockSpec((heads, nkeys, tm), lambda i, k: (0, 0, i))
    return pl.pallas_call(
        functools.partial(_peer_kernel, heads=heads, nkeys=nkeys),
        grid=(t // tm, nk + 1),
        in_specs=[pl.BlockSpec((tm, d), lambda i, k: (i, 0)),
                  pl.BlockSpec((te, d), lambda i, k: (jnp.minimum(k, nk - 1), 0)),
                  pl.BlockSpec((te, d), lambda i, k: (jnp.maximum(k - 1, 0), 0)),
                  fac, fac, pl.BlockSpec((heads, 1, tm), lambda i, k: (0, 0, i))],
        out_specs=pl.BlockSpec((tm, d), lambda i, k: (i, 0)),
        out_shape=jax.ShapeDtypeStruct((t, d), F32),
        scratch_shapes=[pltpu.VMEM((te, tm), BF16), pltpu.VMEM((te, tm), F32),
                        pltpu.VMEM((te, tm), F32)],
        compiler_params=_params("parallel", "arbitrary"),
    )(h2, u, v, e1, e2, pth)


def kernel(x, mem, norm_mix, w_in, w_a_up, b_a, gla_norm, conv_w, w_br_gla, w_br_conv, w_mem_kv,
           w_br_xa, b_gate, w_o, mem_norm, norm_ffn, peer_wq, peer_subkeys, peer_u, peer_v,
           final_norm):
    batch, seq, d = x.shape
    n_mem = mem.shape[1]
    depth = w_in.shape[0]
    rank, qk = w_a_up.shape[1], w_a_up.shape[2]
    vw = w_br_gla.shape[1]
    cw = w_br_conv.shape[1]
    xw = w_br_xa.shape[1]
    t = batch * seq

    a_off = 2 * qk + 2 * vw
    cols_a = {"q": 0, "k": qk, "v": 2 * qk, "r": 2 * qk + vw}
    cols_b = {"cb": 0, "cc": cw, "ch": 2 * cw}
    xq_col = 3 * cw
    gate_col = xq_col + xw
    assert rank <= LANES and a_off + rank + gate_col + 3 * d == w_in.shape[2]

    xs = x.reshape(t, d)
    mem_n = _rmsnorm(mem.reshape(batch * n_mem, d), mem_norm, BF16)

    for l in range(depth):
        wup = jnp.pad(w_a_up[l], ((0, LANES - rank), (0, 0))).astype(BF16)

        hn = _rmsnorm(xs, norm_mix[l], BF16)
        proj_a = _mm(hn, _cast_cols(w_in[l], 0, a_off), BF16)
        proj_b = _mm(hn, _cast_cols(w_in[l], a_off + rank, gate_col + 3 * d), BF16)
        a_lr = _mm(hn, _cast_cols(w_in[l], a_off, LANES), BF16)

        o_gla = _gla(proj_a, a_lr, wup, b_a[l].reshape(1, qk), gla_norm[l].reshape(1, -1),
                     batch=batch, seq=seq, qk=qk, vw=vw, cols=cols_a)
        o_conv = _conv(proj_b, conv_w[l], batch=batch, seq=seq, width=cw, cols=cols_b)
        mem_kv = _mm(mem_n, w_mem_kv[l].astype(BF16), BF16)
        o_xa = _xattn(proj_b, mem_kv, batch=batch, seq=seq, n_mem=n_mem, width=xw, col=xq_col)

        merged = _merge(o_gla, o_conv, o_xa, w_br_gla[l].astype(BF16), w_br_conv[l].astype(BF16),
                        w_br_xa[l].astype(BF16), proj_b, b_gate[l].reshape(3, d), d=d,
                        gate_col=gate_col)
        xs = _mm(merged, w_o[l].astype(BF16), F32, residual=xs)

        h2 = _rmsnorm(xs, norm_ffn[l], BF16)
        q2 = _mm(h2, peer_wq[l].astype(BF16), BF16)
        e1, e2, pth = _route(q2, peer_subkeys[l].astype(BF16))
        y = _peer_mix(h2, peer_u[l].astype(BF16), peer_v[l].astype(BF16), e1, e2, pth)
        if l + 1 < depth:
            xs = xs + y
    return _rmsnorm(xs, final_norm, F32, y=y).reshape(batch, seq, d)
```

```python
import functools
import math

import jax
import jax.numpy as jnp
from jax import lax
from jax.experimental import pallas as pl
from jax.experimental.pallas import tpu as pltpu

F32 = jnp.float32
BF16 = jnp.bfloat16
EPS = 1e-6

GLA_HEADS = 4
GLA_CHUNK = 16
GLA_TAU = 16.0
XA_HEADS = 4
PEER_TOPK = 16

LANES = 128
SUBLANES = 8
VMEM_LIMIT_BYTES = 56 * 1024 * 1024

_NN = (((1,), (0,)), ((), ()))
_NT = (((1,), (1,)), ((), ()))
_TN = (((0,), (0,)), ((), ()))


def _params(*sem):
    return pltpu.CompilerParams(dimension_semantics=sem, vmem_limit_bytes=VMEM_LIMIT_BYTES)


def _tile(n, pref):
    t = min(n, pref)
    while n % t:
        t -= 1
    return t


def _rmsnorm_kernel(x_ref, g_ref, o_ref):
    x = x_ref[...]
    ms = jnp.mean(x * x, axis=-1, keepdims=True)
    o_ref[...] = (x * lax.rsqrt(ms + EPS) * g_ref[...]).astype(o_ref.dtype)


def _add_rmsnorm_kernel(x_ref, y_ref, g_ref, o_ref):
    x = x_ref[...] + y_ref[...]
    ms = jnp.mean(x * x, axis=-1, keepdims=True)
    o_ref[...] = (x * lax.rsqrt(ms + EPS) * g_ref[...]).astype(o_ref.dtype)


def _rmsnorm(x, g, out_dtype, y=None):
    m, d = x.shape
    tm = _tile(m, 256)
    row = pl.BlockSpec((tm, d), lambda i: (i, 0))
    gain = pl.BlockSpec((1, d), lambda i: (0, 0))
    g2 = g.reshape(1, d).astype(F32)
    if y is None:
        body, specs, args = _rmsnorm_kernel, [row, gain], (x, g2)
    else:
        body, specs, args = _add_rmsnorm_kernel, [row, row, gain], (x, y, g2)
    return pl.pallas_call(
        body, grid=(m // tm,), in_specs=specs, out_specs=row,
        out_shape=jax.ShapeDtypeStruct((m, d), out_dtype),
        compiler_params=_params("parallel"))(*args)


def _mm_kernel(x_ref, w_ref, o_ref, *, dims):
    o_ref[...] = lax.dot_general(x_ref[...], w_ref[...], dims,
                                 preferred_element_type=F32).astype(o_ref.dtype)


def _mm_res_kernel(x_ref, w_ref, r_ref, o_ref, *, dims):
    acc = lax.dot_general(x_ref[...], w_ref[...], dims, preferred_element_type=F32)
    o_ref[...] = (r_ref[...] + acc).astype(o_ref.dtype)


def _mm(x, w, out_dtype, residual=None, w_is_nk=False, tm=1024, tn=1024):
    m, k = x.shape
    n = w.shape[0] if w_is_nk else w.shape[1]
    tm, tn = _tile(m, tm), _tile(n, tn)
    xs = pl.BlockSpec((tm, k), lambda i, j: (i, 0))
    if w_is_nk:
        ws, dims = pl.BlockSpec((tn, k), lambda i, j: (j, 0)), _NT
    else:
        ws, dims = pl.BlockSpec((k, tn), lambda i, j: (0, j)), _NN
    os_ = pl.BlockSpec((tm, tn), lambda i, j: (i, j))
    if residual is None:
        body, specs, args = _mm_kernel, [xs, ws], (x, w)
    else:
        body, specs, args = _mm_res_kernel, [xs, ws, os_], (x, w, residual)
    return pl.pallas_call(
        functools.partial(body, dims=dims), grid=(m // tm, n // tn), in_specs=specs,
        out_specs=os_, out_shape=jax.ShapeDtypeStruct((m, n), out_dtype),
        compiler_params=_params("parallel", "parallel"))(*args)


def _cast_rows_kernel(*refs, shift):
    o_ref = refs[-1]
    if shift:
        a_ref, b_ref = refs[0], refs[1]
        o_ref[...] = jnp.concatenate([a_ref[shift:, :], b_ref[:shift, :]],
                                     axis=0).astype(o_ref.dtype)
    else:
        o_ref[...] = refs[0][...].astype(o_ref.dtype)


def _cast_rows(w, start, rows):
    k = w.shape[1]
    tr = _tile(rows, 256)
    shift = start % tr
    assert shift % SUBLANES == 0
    first = start // tr
    specs = [pl.BlockSpec((tr, k), lambda i: (first + i, 0))]
    if shift:
        specs.append(pl.BlockSpec((tr, k), lambda i: (first + i + 1, 0)))
    return pl.pallas_call(
        functools.partial(_cast_rows_kernel, shift=shift), grid=(rows // tr,), in_specs=specs,
        out_specs=pl.BlockSpec((tr, k), lambda i: (i, 0)),
        out_shape=jax.ShapeDtypeStruct((rows, k), BF16),
        compiler_params=_params("parallel"))(*([w] * len(specs)))


def _gla_kernel(q_ref, k_ref, v_ref, r_ref, a_ref, wup_ref, ba_ref, gn_ref, o_ref, b_scr, s_scr,
                *, heads, chunk, tau):
    @pl.when(pl.program_id(1) == 0)
    def _():
        s_scr[...] = jnp.zeros_like(s_scr)

    blk = q_ref.shape[0]
    dk = q_ref.shape[1] // heads
    dv = v_ref.shape[1] // heads
    shift = chunk.bit_length() - 1

    z = jnp.dot(a_ref[...], wup_ref[...], preferred_element_type=F32) + ba_ref[...]
    log_a = -(jnp.maximum(-z, 0.0) + jnp.log1p(jnp.exp(-jnp.abs(z)))) * (1.0 / tau)
    row = lax.broadcasted_iota(jnp.int32, (blk, blk), 0)
    col = lax.broadcasted_iota(jnp.int32, (blk, blk), 1)
    same_chunk = jnp.right_shift(row, shift) == jnp.right_shift(col, shift)
    cum = jnp.where(same_chunk & (col <= row), 1.0, 0.0).astype(BF16)
    hi = log_a.astype(BF16)
    lo = (log_a - hi.astype(F32)).astype(BF16)
    b_scr[...] = (jnp.dot(cum, hi, preferred_element_type=F32)
                  + jnp.dot(cum, lo, preferred_element_type=F32))

    causal = (lax.broadcasted_iota(jnp.int32, (chunk, chunk), 1)
              <= lax.broadcasted_iota(jnp.int32, (chunk, chunk), 0))
    gn = gn_ref[...]

    def body(c, carry):
        rows = pl.ds(pl.multiple_of(c * chunk, chunk), chunk)
        for h in range(heads):
            kc = slice(h * dk, (h + 1) * dk)
            vc = slice(h * dv, (h + 1) * dv)
            b = b_scr[rows, kc]
            b_last = b[chunk - 1:chunk, :]
            q = q_ref[rows, kc].astype(F32)
            k = k_ref[rows, kc].astype(F32)
            v = v_ref[rows, vc]
            q_t = (q * jnp.exp(b) * (dk ** -0.5)).astype(BF16)
            k_in = (k * jnp.exp(-b)).astype(BF16)
            k_out = (k * jnp.exp(b_last - b)).astype(BF16)
            att = lax.dot_general(q_t, k_in, _NT, preferred_element_type=F32)
            att = jnp.where(causal, att, 0.0).astype(BF16)
            st = s_scr[h]
            o = (jnp.dot(att, v, preferred_element_type=F32)
                 + lax.dot_general(q_t, st.astype(BF16), _NT, preferred_element_type=F32))
            s_scr[h] = st * jnp.exp(b_last) + lax.dot_general(
                v, k_out, _TN, preferred_element_type=F32)
            ms = jnp.mean(o * o, axis=-1, keepdims=True)
            y = o * lax.rsqrt(ms + EPS) * gn
            r = r_ref[rows, vc].astype(F32)
            o_ref[rows, vc] = (y * (r * jax.nn.sigmoid(r))).astype(o_ref.dtype)
        return carry

    lax.fori_loop(0, blk // chunk, body, 0)


def _gla(proj, a_lr, wup, ba, gn, *, batch, seq, qk, vw, cols):
    blk = _tile(seq, 256)
    nb = seq // blk
    dv = vw // GLA_HEADS

    def at(width, off):
        return pl.BlockSpec((blk, width), lambda b, j: (b * nb + j, off // width))

    full = lambda shape: pl.BlockSpec(shape, lambda b, j: (0,) * len(shape))
    return pl.pallas_call(
        functools.partial(_gla_kernel, heads=GLA_HEADS, chunk=GLA_CHUNK, tau=GLA_TAU),
        grid=(batch, nb),
        in_specs=[at(qk, cols["q"]), at(qk, cols["k"]), at(vw, cols["v"]), at(vw, cols["r"]),
                  pl.BlockSpec((blk, a_lr.shape[1]), lambda b, j: (b * nb + j, 0)),
                  full(wup.shape), full(ba.shape), full(gn.shape)],
        out_specs=pl.BlockSpec((blk, vw), lambda b, j: (b * nb + j, 0)),
        out_shape=jax.ShapeDtypeStruct((batch * seq, vw), BF16),
        scratch_shapes=[pltpu.VMEM((blk, qk), F32),
                        pltpu.VMEM((GLA_HEADS, dv, qk // GLA_HEADS), F32)],
        compiler_params=_params("parallel", "arbitrary"),
    )(proj, proj, proj, proj, a_lr, wup, ba, gn)


def _conv_kernel(cb_ref, cc_ref, ch_ref, pc_ref, ph_ref, w_ref, o_ref, u_scr):
    ts = cc_ref.shape[0]
    halo = pc_ref.shape[0]
    u = cc_ref[...].astype(F32) * ch_ref[...].astype(F32)
    prev = pc_ref[...].astype(F32) * ph_ref[...].astype(F32)
    u_scr[0:halo, :] = jnp.where(pl.program_id(1) > 0, prev, 0.0)
    u_scr[halo:halo + ts, :] = u
    w = w_ref[...]
    y = (w[0:1] * u + w[1:2] * u_scr[halo - 1:halo - 1 + ts, :]
         + w[2:3] * u_scr[halo - 2:halo - 2 + ts, :])
    o_ref[...] = (cb_ref[...].astype(F32) * y).astype(o_ref.dtype)


def _conv(proj, conv_w, *, batch, seq, width, cols):
    ts = _tile(seq, 512)
    tw = _tile(width, 1024)
    halo = 16
    ns, nw = seq // ts, width // tw

    def at(off):
        return pl.BlockSpec((ts, tw), lambda b, j, c: (b * ns + j, off // tw + c))

    def halo_at(off):
        return pl.BlockSpec(
            (halo, tw),
            lambda b, j, c: (jnp.maximum((b * seq + j * ts) // halo - 1, 0), off // tw + c))

    return pl.pallas_call(
        _conv_kernel, grid=(batch, ns, nw),
        in_specs=[at(cols["cb"]), at(cols["cc"]), at(cols["ch"]),
                  halo_at(cols["cc"]), halo_at(cols["ch"]),
                  pl.BlockSpec((conv_w.shape[0], tw), lambda b, j, c: (0, c))],
        out_specs=pl.BlockSpec((ts, tw), lambda b, j, c: (b * ns + j, c)),
        out_shape=jax.ShapeDtypeStruct((batch * seq, width), BF16),
        scratch_shapes=[pltpu.VMEM((halo + ts, tw), F32)],
        compiler_params=_params("parallel", "parallel", "parallel"),
    )(proj, proj, proj, proj, proj, conv_w)


def _xa_kernel(q_ref, k_ref, v_ref, o_ref, *, heads):
    dh = q_ref.shape[1] // heads
    for h in range(heads):
        cs = slice(h * dh, (h + 1) * dh)
        s = lax.dot_general(q_ref[:, cs], k_ref[:, cs], _NT,
                            preferred_element_type=F32) * (dh ** -0.5)
        p = jnp.exp(s - jnp.max(s, axis=-1, keepdims=True))
        l = jnp.sum(p, axis=-1, keepdims=True)
        o = jnp.dot(p.astype(BF16), v_ref[:, cs], preferred_element_type=F32)
        o_ref[:, cs] = (o / l).astype(o_ref.dtype)


def _xattn(proj, mem_kv, *, batch, seq, n_mem, width, col):
    tq = _tile(seq, 512)
    nq = seq // tq
    return pl.pallas_call(
        functools.partial(_xa_kernel, heads=XA_HEADS), grid=(batch, nq),
        in_specs=[pl.BlockSpec((tq, width), lambda b, j: (b * nq + j, col // width)),
                  pl.BlockSpec((n_mem, width), lambda b, j: (b, 0)),
                  pl.BlockSpec((n_mem, width), lambda b, j: (b, 1))],
        out_specs=pl.BlockSpec((tq, width), lambda b, j: (b * nq + j, 0)),
        out_shape=jax.ShapeDtypeStruct((batch * seq, width), BF16),
        compiler_params=_params("parallel", "parallel"),
    )(proj, mem_kv, mem_kv)


def _merge_kernel(og_ref, oc_ref, ox_ref, wg_ref, wc_ref, wx_ref, g0_ref, g1_ref, g2_ref,
                  bg_ref, o_ref):
    acc = None
    for idx, (a_ref, w_ref, g_ref) in enumerate(
            ((og_ref, wg_ref, g0_ref), (oc_ref, wc_ref, g1_ref), (ox_ref, wx_ref, g2_ref))):
        y = jnp.dot(a_ref[...], w_ref[...], preferred_element_type=F32)
        g = jax.nn.sigmoid(g_ref[...].astype(F32) + bg_ref[idx:idx + 1, :])
        acc = g * y if acc is None else acc + g * y
    o_ref[...] = acc.astype(o_ref.dtype)


def _merge(o_gla, o_conv, o_xa, w_gla, w_conv, w_xa, proj, b_gate, *, d, gate_col):
    m = o_gla.shape[0]
    tm, tn = _tile(m, 512), _tile(d, 1024)

    def lhs(a):
        return pl.BlockSpec((tm, a.shape[1]), lambda i, j: (i, 0))

    def rhs(w):
        return pl.BlockSpec((w.shape[0], tn), lambda i, j: (0, j))

    def gate(g):
        return pl.BlockSpec((tm, tn), lambda i, j: (i, (gate_col + g * d) // tn + j))

    return pl.pallas_call(
        _merge_kernel, grid=(m // tm, d // tn),
        in_specs=[lhs(o_gla), lhs(o_conv), lhs(o_xa), rhs(w_gla), rhs(w_conv), rhs(w_xa),
                  gate(0), gate(1), gate(2),
                  pl.BlockSpec((b_gate.shape[0], tn), lambda i, j: (0, j))],
        out_specs=pl.BlockSpec((tm, tn), lambda i, j: (i, j)),
        out_shape=jax.ShapeDtypeStruct((m, d), BF16),
        compiler_params=_params("parallel", "parallel"),
    )(o_gla, o_conv, o_xa, w_gla, w_conv, w_xa, proj, proj, proj, b_gate)


def _n_candidates(topk):
    return sum(topk // p for p in range(1, topk + 1))


def _route_kernel(q_ref, sk_ref, e1_ref, e2_ref, pth_ref, a_scr, b_scr, c_scr, *, topk):
    dq = sk_ref.shape[3]

    def half(c, top_scr):
        s = lax.dot_general(sk_ref[0, c], q_ref[:, c * dq:(c + 1) * dq], _NT,
                            preferred_element_type=F32)
        e = jnp.exp(s - jnp.max(s, axis=0, keepdims=True))
        top = jnp.max(e, axis=0, keepdims=True)
        top_scr[0:1, :] = top
        for r in range(1, topk):
            top = jnp.max(jnp.where(e < top, e, -1.0), axis=0, keepdims=True)
            top_scr[r:r + 1, :] = jnp.maximum(top, 0.0)
        return e

    e1 = half(0, a_scr)
    e2 = half(1, b_scr)

    def candidates(scale):
        off = 0
        for p in range(topk):
            nq = topk // (p + 1)
            c_scr[off:off + nq, :] = a_scr[p:p + 1, :] * (b_scr[0:nq, :] * scale)
            off += nq
        if off < c_scr.shape[0]:
            c_scr[off:, :] = jnp.full((c_scr.shape[0] - off, c_scr.shape[1]), -1.0, F32)
        return c_scr[...]

    cand = candidates(1.0)
    kth = jnp.max(cand, axis=0, keepdims=True)
    for r in range(1, topk):
        kth = jnp.max(jnp.where(cand < kth, cand, -1.0), axis=0, keepdims=True)
    kth = jnp.maximum(kth, 0.0)
    sel = cand >= kth
    inv_z = 1.0 / jnp.sum(jnp.where(sel, cand, 0.0), axis=0, keepdims=True)
    cand_n = candidates(inv_z)
    e1_ref[0] = e1
    e2_ref[0] = e2 * inv_z
    pth_ref[0] = jnp.min(jnp.where(sel, cand_n, jnp.inf), axis=0, keepdims=True)


def _route(q2, sub_keys):
    t = q2.shape[0]
    heads, _, nkeys, dq = sub_keys.shape
    tm = _tile(t, 512)
    ncand = -(-_n_candidates(PEER_TOPK) // 8) * 8
    fac = pl.BlockSpec((1, nkeys, tm), lambda i, h: (h, 0, i))
    return pl.pallas_call(
        functools.partial(_route_kernel, topk=PEER_TOPK), grid=(t // tm, heads),
        in_specs=[pl.BlockSpec((tm, 2 * dq), lambda i, h: (i, h)),
                  pl.BlockSpec((1, 2, nkeys, dq), lambda i, h: (h, 0, 0, 0))],
        out_specs=[fac, fac, pl.BlockSpec((1, 1, tm), lambda i, h: (h, 0, i))],
        out_shape=[jax.ShapeDtypeStruct((heads, nkeys, t), F32),
                   jax.ShapeDtypeStruct((heads, nkeys, t), F32),
                   jax.ShapeDtypeStruct((heads, 1, t), F32)],
        scratch_shapes=[pltpu.VMEM((PEER_TOPK, tm), F32), pltpu.VMEM((PEER_TOPK, tm), F32),
                        pltpu.VMEM((ncand, tm), F32)],
        compiler_params=_params("parallel", "parallel"),
    )(q2, sub_keys)


def _peer_kernel(h_ref, u_ref, v_ref, e1_ref, e2_ref, pth_ref, o_ref, w_scr, *, heads, nkeys):
    kstep = pl.program_id(1)
    tm = h_ref.shape[0]
    te = u_ref.shape[0]
    ni = te // nkeys

    @pl.when(kstep == 0)
    def _():
        o_ref[...] = jnp.zeros_like(o_ref)

    parts = SUBLANES // ni
    group = pl.ds(pl.multiple_of((kstep // parts) * SUBLANES, SUBLANES), SUBLANES)
    part = kstep % parts

    def e1_row(h, ii, ls):
        grp = e1_ref[h, group, ls]
        row = grp[ii:ii + 1]
        for s in range(1, parts):
            row = jnp.where(part == s, grp[s * ni + ii:s * ni + ii + 1], row)
        return row

    a_t = lax.dot_general(u_ref[...], h_ref[...], _NT, preferred_element_type=F32)
    for ii in range(ni):
        rs = slice(ii * nkeys, (ii + 1) * nkeys)
        for tc in range(tm // LANES):
            ls = slice(tc * LANES, (tc + 1) * LANES)
            g = jnp.zeros((nkeys, LANES), F32)
            for h in range(heads):
                p = e1_row(h, ii, ls) * e2_ref[h, :, ls]
                g = g + jnp.where(p >= pth_ref[h, :, ls], p, 0.0)
            a = a_t[rs, ls]
            gelu = 0.5 * a * (1.0 + lax.erf(a * (1.0 / math.sqrt(2.0))))
            w_scr[rs, ls] = (g * gelu).astype(BF16)
    o_ref[...] += lax.dot_general(w_scr[...], v_ref[...], _TN, preferred_element_type=F32)


def _peer_mix(h2, u, v, e1, e2, pth):
    t, d = h2.shape
    n_exp = u.shape[0]
    heads, nkeys, _ = e1.shape
    tm, te = _tile(t, 512), _tile(n_exp, 512)
    fac = pl.BlockSpec((heads, nkeys, tm), lambda i, k: (0, 0, i))
    return pl.pallas_call(
        functools.partial(_peer_kernel, heads=heads, nkeys=nkeys),
        grid=(t // tm, n_exp // te),
        in_specs=[pl.BlockSpec((tm, d), lambda i, k: (i, 0)),
                  pl.BlockSpec((te, d), lambda i, k: (k, 0)),
                  pl.BlockSpec((te, d), lambda i, k: (k, 0)),
                  fac, fac, pl.BlockSpec((heads, 1, tm), lambda i, k: (0, 0, i))],
        out_specs=pl.BlockSpec((tm, d), lambda i, k: (i, 0)),
        out_shape=jax.ShapeDtypeStruct((t, d), F32),
        scratch_shapes=[pltpu.VMEM((te, tm), BF16)],
        compiler_params=_params("parallel", "arbitrary"),
    )(h2, u, v, e1, e2, pth)


def kernel(x, mem, norm_mix, w_in, w_a_up, b_a, gla_norm, conv_w, w_br_gla, w_br_conv, w_mem_kv,
           w_br_xa, b_gate, w_o, mem_norm, norm_ffn, peer_wq, peer_subkeys, peer_u, peer_v,
           final_norm):
    batch, seq, d = x.shape
    n_mem = mem.shape[1]
    depth = w_in.shape[0]
    rank, qk = w_a_up.shape[1], w_a_up.shape[2]
    vw = w_br_gla.shape[1]
    cw = w_br_conv.shape[1]
    xw = w_br_xa.shape[1]
    t = batch * seq

    a_off = 2 * qk + 2 * vw
    cols_a = {"q": 0, "k": qk, "v": 2 * qk, "r": 2 * qk + vw}
    cols_b = {"cb": 0, "cc": cw, "ch": 2 * cw}
    xq_col = 3 * cw
    gate_col = xq_col + xw
    assert rank <= LANES and a_off + rank + gate_col + 3 * d == w_in.shape[2]

    xs = x.reshape(t, d)
    mem_n = _rmsnorm(mem.reshape(batch * n_mem, d), mem_norm, BF16)

    for l in range(depth):
        wup = jnp.pad(w_a_up[l], ((0, LANES - rank), (0, 0))).astype(BF16)

        hn = _rmsnorm(xs, norm_mix[l], BF16)
        w_t = jnp.swapaxes(w_in[l], 0, 1)
        proj_a = _mm(hn, _cast_rows(w_t, 0, a_off), BF16, w_is_nk=True)
        proj_b = _mm(hn, _cast_rows(w_t, a_off + rank, gate_col + 3 * d), BF16, w_is_nk=True)
        a_lr = _mm(hn, _cast_rows(w_t, a_off, LANES), BF16, w_is_nk=True)

        o_gla = _gla(proj_a, a_lr, wup, b_a[l].reshape(1, qk), gla_norm[l].reshape(1, -1),
                     batch=batch, seq=seq, qk=qk, vw=vw, cols=cols_a)
        o_conv = _conv(proj_b, conv_w[l], batch=batch, seq=seq, width=cw, cols=cols_b)
        mem_kv = _mm(mem_n, w_mem_kv[l].astype(BF16), BF16)
        o_xa = _xattn(proj_b, mem_kv, batch=batch, seq=seq, n_mem=n_mem, width=xw, col=xq_col)

        merged = _merge(o_gla, o_conv, o_xa, w_br_gla[l].astype(BF16), w_br_conv[l].astype(BF16),
                        w_br_xa[l].astype(BF16), proj_b, b_gate[l].reshape(3, d), d=d,
                        gate_col=gate_col)
        xs = _mm(merged, w_o[l].astype(BF16), F32, residual=xs)

        h2 = _rmsnorm(xs, norm_ffn[l], BF16)
        q2 = _mm(h2, peer_wq[l].astype(BF16), BF16)
        e1, e2, pth = _route(q2, peer_subkeys[l].astype(BF16))
        y = _peer_mix(h2, peer_u[l].astype(BF16), peer_v[l].astype(BF16), e1, e2, pth)
        if l + 1 < depth:
            xs = xs + y
    return _rmsnorm(xs, final_norm, F32, y=y).reshape(batch, seq, d)
```

```python
import functools
import math

import jax
import jax.numpy as jnp
from jax import lax
from jax.experimental import pallas as pl
from jax.experimental.pallas import tpu as pltpu

F32 = jnp.float32
BF16 = jnp.bfloat16
EPS = 1e-6

GLA_HEADS = 4
GLA_CHUNK = 16
GLA_TAU = 16.0
XA_HEADS = 4
PEER_TOPK = 16

LANES = 128
SUBLANES = 8
VMEM_LIMIT_BYTES = 56 * 1024 * 1024

_NN = (((1,), (0,)), ((), ()))
_NT = (((1,), (1,)), ((), ()))
_TN = (((0,), (0,)), ((), ()))


def _params(*sem):
    return pltpu.CompilerParams(dimension_semantics=sem, vmem_limit_bytes=VMEM_LIMIT_BYTES)


def _tile(n, pref):
    t = min(n, pref)
    while n % t:
        t -= 1
    return t


def _rmsnorm_kernel(x_ref, g_ref, o_ref):
    x = x_ref[...]
    ms = jnp.mean(x * x, axis=-1, keepdims=True)
    o_ref[...] = (x * lax.rsqrt(ms + EPS) * g_ref[...]).astype(o_ref.dtype)


def _add_rmsnorm_kernel(x_ref, y_ref, g_ref, o_ref):
    x = x_ref[...] + y_ref[...]
    ms = jnp.mean(x * x, axis=-1, keepdims=True)
    o_ref[...] = (x * lax.rsqrt(ms + EPS) * g_ref[...]).astype(o_ref.dtype)


def _rmsnorm(x, g, out_dtype, y=None):
    m, d = x.shape
    tm = _tile(m, 256)
    row = pl.BlockSpec((tm, d), lambda i: (i, 0))
    gain = pl.BlockSpec((1, d), lambda i: (0, 0))
    g2 = g.reshape(1, d).astype(F32)
    if y is None:
        body, specs, args = _rmsnorm_kernel, [row, gain], (x, g2)
    else:
        body, specs, args = _add_rmsnorm_kernel, [row, row, gain], (x, y, g2)
    return pl.pallas_call(
        body, grid=(m // tm,), in_specs=specs, out_specs=row,
        out_shape=jax.ShapeDtypeStruct((m, d), out_dtype),
        compiler_params=_params("parallel"))(*args)


def _mm_kernel(x_ref, w_ref, o_ref, *, dims):
    o_ref[...] = lax.dot_general(x_ref[...], w_ref[...], dims,
                                 preferred_element_type=F32).astype(o_ref.dtype)


def _mm_res_kernel(x_ref, w_ref, r_ref, o_ref, *, dims):
    acc = lax.dot_general(x_ref[...], w_ref[...], dims, preferred_element_type=F32)
    o_ref[...] = (r_ref[...] + acc).astype(o_ref.dtype)


def _mm(x, w, out_dtype, residual=None, w_is_nk=False, tm=1024, tn=1024):
    m, k = x.shape
    n = w.shape[0] if w_is_nk else w.shape[1]
    tm, tn = _tile(m, tm), _tile(n, tn)
    xs = pl.BlockSpec((tm, k), lambda i, j: (i, 0))
    if w_is_nk:
        ws, dims = pl.BlockSpec((tn, k), lambda i, j: (j, 0)), _NT
    else:
        ws, dims = pl.BlockSpec((k, tn), lambda i, j: (0, j)), _NN
    os_ = pl.BlockSpec((tm, tn), lambda i, j: (i, j))
    if residual is None:
        body, specs, args = _mm_kernel, [xs, ws], (x, w)
    else:
        body, specs, args = _mm_res_kernel, [xs, ws, os_], (x, w, residual)
    return pl.pallas_call(
        functools.partial(body, dims=dims), grid=(m // tm, n // tn), in_specs=specs,
        out_specs=os_, out_shape=jax.ShapeDtypeStruct((m, n), out_dtype),
        compiler_params=_params("parallel", "parallel"))(*args)


def _cast_rows_kernel(*refs, shift):
    o_ref = refs[-1]
    if shift:
        a_ref, b_ref = refs[0], refs[1]
        o_ref[...] = jnp.concatenate([a_ref[shift:, :], b_ref[:shift, :]],
                                     axis=0).astype(o_ref.dtype)
    else:
        o_ref[...] = refs[0][...].astype(o_ref.dtype)


def _cast_rows(w, start, rows):
    k = w.shape[1]
    tr = _tile(rows, 256)
    shift = start % tr
    assert shift % SUBLANES == 0
    first = start // tr
    specs = [pl.BlockSpec((tr, k), lambda i: (first + i, 0))]
    if shift:
        specs.append(pl.BlockSpec((tr, k), lambda i: (first + i + 1, 0)))
    return pl.pallas_call(
        functools.partial(_cast_rows_kernel, shift=shift), grid=(rows // tr,), in_specs=specs,
        out_specs=pl.BlockSpec((tr, k), lambda i: (i, 0)),
        out_shape=jax.ShapeDtypeStruct((rows, k), BF16),
        compiler_params=_params("parallel"))(*([w] * len(specs)))


def _gla_kernel(q_ref, k_ref, v_ref, r_ref, a_ref, wup_ref, ba_ref, gn_ref, o_ref, b_scr, s_scr,
                *, heads, chunk, tau):
    @pl.when(pl.program_id(1) == 0)
    def _():
        s_scr[...] = jnp.zeros_like(s_scr)

    blk = q_ref.shape[0]
    dk = q_ref.shape[1] // heads
    dv = v_ref.shape[1] // heads
    shift = chunk.bit_length() - 1

    z = jnp.dot(a_ref[...], wup_ref[...], preferred_element_type=F32) + ba_ref[...]
    log_a = -(jnp.maximum(-z, 0.0) + jnp.log1p(jnp.exp(-jnp.abs(z)))) * (1.0 / tau)
    row = lax.broadcasted_iota(jnp.int32, (blk, blk), 0)
    col = lax.broadcasted_iota(jnp.int32, (blk, blk), 1)
    same_chunk = jnp.right_shift(row, shift) == jnp.right_shift(col, shift)
    cum = jnp.where(same_chunk & (col <= row), 1.0, 0.0).astype(BF16)
    hi = log_a.astype(BF16)
    lo = (log_a - hi.astype(F32)).astype(BF16)
    b_scr[...] = (jnp.dot(cum, hi, preferred_element_type=F32)
                  + jnp.dot(cum, lo, preferred_element_type=F32))

    pair = 2 * chunk
    trow = lax.broadcasted_iota(jnp.int32, (pair, pair), 0)
    tcol = lax.broadcasted_iota(jnp.int32, (pair, pair), 1)
    intra = ((trow < chunk) == (tcol < chunk)) & (tcol <= trow)
    cross = (trow >= chunk) & (tcol < chunk)
    in_a = lax.broadcasted_iota(jnp.int32, (pair, 1), 0) < chunk
    gn = gn_ref[...]

    def body(c, carry):
        rows = pl.ds(pl.multiple_of(c * pair, pair), pair)
        for h in range(heads):
            kc = slice(h * dk, (h + 1) * dk)
            vc = slice(h * dv, (h + 1) * dv)
            b = b_scr[rows, kc]
            last_a = b[chunk - 1:chunk, :]
            last_b = b[pair - 1:pair, :]
            dec_a = jnp.exp(last_a)
            dec_b = jnp.exp(last_b)
            q = q_ref[rows, kc].astype(F32)
            k = k_ref[rows, kc].astype(F32)
            v = v_ref[rows, vc]
            q_t = q * jnp.exp(b) * (dk ** -0.5)
            k_in = (k * jnp.exp(-b)).astype(BF16)
            k_out = k * jnp.exp(jnp.where(in_a, last_a, last_b) - b)
            q_b = q_t.astype(BF16)
            att_in = lax.dot_general(q_b, k_in, _NT, preferred_element_type=F32)
            att_x = lax.dot_general(q_b, k_out.astype(BF16), _NT, preferred_element_type=F32)
            att = jnp.where(intra, att_in, jnp.where(cross, att_x, 0.0)).astype(BF16)
            st = s_scr[h]
            q_s = (q_t * jnp.where(in_a, 1.0, dec_a)).astype(BF16)
            o = (jnp.dot(att, v, preferred_element_type=F32)
                 + lax.dot_general(q_s, st.astype(BF16), _NT, preferred_element_type=F32))
            k_s = (k_out * jnp.where(in_a, dec_b, 1.0)).astype(BF16)
            s_scr[h] = st * (dec_a * dec_b) + lax.dot_general(
                v, k_s, _TN, preferred_element_type=F32)
            ms = jnp.mean(o * o, axis=-1, keepdims=True)
            y = o * lax.rsqrt(ms + EPS) * gn
            r = r_ref[rows, vc].astype(F32)
            o_ref[rows, vc] = (y * (r * jax.nn.sigmoid(r))).astype(o_ref.dtype)
        return carry

    lax.fori_loop(0, blk // pair, body, 0)


def _gla(proj, a_lr, wup, ba, gn, *, batch, seq, qk, vw, cols):
    blk = _tile(seq, 256)
    nb = seq // blk
    dv = vw // GLA_HEADS

    def at(width, off):
        return pl.BlockSpec((blk, width), lambda b, j: (b * nb + j, off // width))

    full = lambda shape: pl.BlockSpec(shape, lambda b, j: (0,) * len(shape))
    return pl.pallas_call(
        functools.partial(_gla_kernel, heads=GLA_HEADS, chunk=GLA_CHUNK, tau=GLA_TAU),
        grid=(batch, nb),
        in_specs=[at(qk, cols["q"]), at(qk, cols["k"]), at(vw, cols["v"]), at(vw, cols["r"]),
                  pl.BlockSpec((blk, a_lr.shape[1]), lambda b, j: (b * nb + j, 0)),
                  full(wup.shape), full(ba.shape), full(gn.shape)],
        out_specs=pl.BlockSpec((blk, vw), lambda b, j: (b * nb + j, 0)),
        out_shape=jax.ShapeDtypeStruct((batch * seq, vw), BF16),
        scratch_shapes=[pltpu.VMEM((blk, qk), F32),
                        pltpu.VMEM((GLA_HEADS, dv, qk // GLA_HEADS), F32)],
        compiler_params=_params("parallel", "arbitrary"),
    )(proj, proj, proj, proj, a_lr, wup, ba, gn)


def _conv_kernel(cb_ref, cc_ref, ch_ref, pc_ref, ph_ref, w_ref, o_ref, u_scr):
    ts = cc_ref.shape[0]
    halo = pc_ref.shape[0]
    u = cc_ref[...].astype(F32) * ch_ref[...].astype(F32)
    prev = pc_ref[...].astype(F32) * ph_ref[...].astype(F32)
    u_scr[0:halo, :] = jnp.where(pl.program_id(1) > 0, prev, 0.0)
    u_scr[halo:halo + ts, :] = u
    w = w_ref[...]
    y = (w[0:1] * u + w[1:2] * u_scr[halo - 1:halo - 1 + ts, :]
         + w[2:3] * u_scr[halo - 2:halo - 2 + ts, :])
    o_ref[...] = (cb_ref[...].astype(F32) * y).astype(o_ref.dtype)


def _conv(proj, conv_w, *, batch, seq, width, cols):
    ts = _tile(seq, 512)
    tw = _tile(width, 1024)
    halo = 16
    ns, nw = seq // ts, width // tw

    def at(off):
        return pl.BlockSpec((ts, tw), lambda b, j, c: (b * ns + j, off // tw + c))

    def halo_at(off):
        return pl.BlockSpec(
            (halo, tw),
            lambda b, j, c: (jnp.maximum((b * seq + j * ts) // halo - 1, 0), off // tw + c))

    return pl.pallas_call(
        _conv_kernel, grid=(batch, ns, nw),
        in_specs=[at(cols["cb"]), at(cols["cc"]), at(cols["ch"]),
                  halo_at(cols["cc"]), halo_at(cols["ch"]),
                  pl.BlockSpec((conv_w.shape[0], tw), lambda b, j, c: (0, c))],
        out_specs=pl.BlockSpec((ts, tw), lambda b, j, c: (b * ns + j, c)),
        out_shape=jax.ShapeDtypeStruct((batch * seq, width), BF16),
        scratch_shapes=[pltpu.VMEM((halo + ts, tw), F32)],
        compiler_params=_params("parallel", "parallel", "parallel"),
    )(proj, proj, proj, proj, proj, conv_w)


def _xa_kernel(q_ref, k_ref, v_ref, o_ref, *, heads):
    dh = q_ref.shape[1] // heads
    for h in range(heads):
        cs = slice(h * dh, (h + 1) * dh)
        s = lax.dot_general(q_ref[:, cs], k_ref[:, cs], _NT,
                            preferred_element_type=F32) * (dh ** -0.5)
        p = jnp.exp(s - jnp.max(s, axis=-1, keepdims=True))
        l = jnp.sum(p, axis=-1, keepdims=True)
        o = jnp.dot(p.astype(BF16), v_ref[:, cs], preferred_element_type=F32)
        o_ref[:, cs] = (o / l).astype(o_ref.dtype)


def _xattn(proj, mem_kv, *, batch, seq, n_mem, width, col):
    tq = _tile(seq, 512)
    nq = seq // tq
    return pl.pallas_call(
        functools.partial(_xa_kernel, heads=XA_HEADS), grid=(batch, nq),
        in_specs=[pl.BlockSpec((tq, width), lambda b, j: (b * nq + j, col // width)),
                  pl.BlockSpec((n_mem, width), lambda b, j: (b, 0)),
                  pl.BlockSpec((n_mem, width), lambda b, j: (b, 1))],
        out_specs=pl.BlockSpec((tq, width), lambda b, j: (b * nq + j, 0)),
        out_shape=jax.ShapeDtypeStruct((batch * seq, width), BF16),
        compiler_params=_params("parallel", "parallel"),
    )(proj, mem_kv, mem_kv)


def _merge_kernel(og_ref, oc_ref, ox_ref, wg_ref, wc_ref, wx_ref, g0_ref, g1_ref, g2_ref,
                  bg_ref, o_ref):
    acc = None
    for idx, (a_ref, w_ref, g_ref) in enumerate(
            ((og_ref, wg_ref, g0_ref), (oc_ref, wc_ref, g1_ref), (ox_ref, wx_ref, g2_ref))):
        y = jnp.dot(a_ref[...], w_ref[...], preferred_element_type=F32)
        g = jax.nn.sigmoid(g_ref[...].astype(F32) + bg_ref[idx:idx + 1, :])
        acc = g * y if acc is None else acc + g * y
    o_ref[...] = acc.astype(o_ref.dtype)


def _merge(o_gla, o_conv, o_xa, w_gla, w_conv, w_xa, proj, b_gate, *, d, gate_col):
    m = o_gla.shape[0]
    tm, tn = _tile(m, 512), _tile(d, 1024)

    def lhs(a):
        return pl.BlockSpec((tm, a.shape[1]), lambda i, j: (i, 0))

    def rhs(w):
        return pl.BlockSpec((w.shape[0], tn), lambda i, j: (0, j))

    def gate(g):
        return pl.BlockSpec((tm, tn), lambda i, j: (i, (gate_col + g * d) // tn + j))

    return pl.pallas_call(
        _merge_kernel, grid=(m // tm, d // tn),
        in_specs=[lhs(o_gla), lhs(o_conv), lhs(o_xa), rhs(w_gla), rhs(w_conv), rhs(w_xa),
                  gate(0), gate(1), gate(2),
                  pl.BlockSpec((b_gate.shape[0], tn), lambda i, j: (0, j))],
        out_specs=pl.BlockSpec((tm, tn), lambda i, j: (i, j)),
        out_shape=jax.ShapeDtypeStruct((m, d), BF16),
        compiler_params=_params("parallel", "parallel"),
    )(o_gla, o_conv, o_xa, w_gla, w_conv, w_xa, proj, proj, proj, b_gate)


def _n_candidates(topk):
    return sum(topk // p for p in range(1, topk + 1))


def _route_kernel(q_ref, sk_ref, e1_ref, e2_ref, pth_ref, a_scr, b_scr, c_scr, *, topk):
    dq = sk_ref.shape[3]

    def half(c, top_scr):
        s = lax.dot_general(sk_ref[0, c], q_ref[:, c * dq:(c + 1) * dq], _NT,
                            preferred_element_type=F32)
        e = jnp.exp(s - jnp.max(s, axis=0, keepdims=True))
        top = jnp.max(e, axis=0, keepdims=True)
        top_scr[0:1, :] = top
        for r in range(1, topk):
            top = jnp.max(jnp.where(e < top, e, -1.0), axis=0, keepdims=True)
            top_scr[r:r + 1, :] = jnp.maximum(top, 0.0)
        return e

    e1 = half(0, a_scr)
    e2 = half(1, b_scr)

    def candidates(scale):
        off = 0
        for p in range(topk):
            nq = topk // (p + 1)
            c_scr[off:off + nq, :] = a_scr[p:p + 1, :] * (b_scr[0:nq, :] * scale)
            off += nq
        if off < c_scr.shape[0]:
            c_scr[off:, :] = jnp.full((c_scr.shape[0] - off, c_scr.shape[1]), -1.0, F32)
        return c_scr[...]

    cand = candidates(1.0)
    kth = jnp.max(cand, axis=0, keepdims=True)
    for r in range(1, topk):
        kth = jnp.max(jnp.where(cand < kth, cand, -1.0), axis=0, keepdims=True)
    kth = jnp.maximum(kth, 0.0)
    sel = cand >= kth
    inv_z = 1.0 / jnp.sum(jnp.where(sel, cand, 0.0), axis=0, keepdims=True)
    cand_n = candidates(inv_z)
    e1_ref[0] = e1
    e2_ref[0] = e2 * inv_z
    pth_ref[0] = jnp.min(jnp.where(sel, cand_n, jnp.inf), axis=0, keepdims=True)


def _route(q2, sub_keys):
    t = q2.shape[0]
    heads, _, nkeys, dq = sub_keys.shape
    tm = _tile(t, 512)
    ncand = -(-_n_candidates(PEER_TOPK) // 8) * 8
    fac = pl.BlockSpec((1, nkeys, tm), lambda i, h: (h, 0, i))
    return pl.pallas_call(
        functools.partial(_route_kernel, topk=PEER_TOPK), grid=(t // tm, heads),
        in_specs=[pl.BlockSpec((tm, 2 * dq), lambda i, h: (i, h)),
                  pl.BlockSpec((1, 2, nkeys, dq), lambda i, h: (h, 0, 0, 0))],
        out_specs=[fac, fac, pl.BlockSpec((1, 1, tm), lambda i, h: (h, 0, i))],
        out_shape=[jax.ShapeDtypeStruct((heads, nkeys, t), F32),
                   jax.ShapeDtypeStruct((heads, nkeys, t), F32),
                   jax.ShapeDtypeStruct((heads, 1, t), F32)],
        scratch_shapes=[pltpu.VMEM((PEER_TOPK, tm), F32), pltpu.VMEM((PEER_TOPK, tm), F32),
                        pltpu.VMEM((ncand, tm), F32)],
        compiler_params=_params("parallel", "parallel"),
    )(q2, sub_keys)


def _peer_kernel(h_ref, u_ref, v_ref, e1_ref, e2_ref, pth_ref, o_ref, w_scr, *, heads, nkeys):
    kstep = pl.program_id(1)
    tm = h_ref.shape[0]
    te = u_ref.shape[0]
    ni = te // nkeys

    @pl.when(kstep == 0)
    def _():
        o_ref[...] = jnp.zeros_like(o_ref)

    parts = SUBLANES // ni
    group = pl.ds(pl.multiple_of((kstep // parts) * SUBLANES, SUBLANES), SUBLANES)
    part = kstep % parts

    def e1_row(h, ii, ls):
        grp = e1_ref[h, group, ls]
        row = grp[ii:ii + 1]
        for s in range(1, parts):
            row = jnp.where(part == s, grp[s * ni + ii:s * ni + ii + 1], row)
        return row

    a_t = lax.dot_general(u_ref[...], h_ref[...], _NT, preferred_element_type=F32)
    for ii in range(ni):
        rs = slice(ii * nkeys, (ii + 1) * nkeys)
        for tc in range(tm // LANES):
            ls = slice(tc * LANES, (tc + 1) * LANES)
            g = jnp.zeros((nkeys, LANES), F32)
            for h in range(heads):
                p = e1_row(h, ii, ls) * e2_ref[h, :, ls]
                g = g + jnp.where(p >= pth_ref[h, :, ls], p, 0.0)
            a = a_t[rs, ls]
            gelu = 0.5 * a * (1.0 + lax.erf(a * (1.0 / math.sqrt(2.0))))
            w_scr[rs, ls] = (g * gelu).astype(BF16)
    o_ref[...] += lax.dot_general(w_scr[...], v_ref[...], _TN, preferred_element_type=F32)


def _peer_mix(h2, u, v, e1, e2, pth):
    t, d = h2.shape
    n_exp = u.shape[0]
    heads, nkeys, _ = e1.shape
    tm, te = _tile(t, 512), _tile(n_exp, 512)
    fac = pl.BlockSpec((heads, nkeys, tm), lambda i, k: (0, 0, i))
    return pl.pallas_call(
        functools.partial(_peer_kernel, heads=heads, nkeys=nkeys),
        grid=(t // tm, n_exp // te),
        in_specs=[pl.BlockSpec((tm, d), lambda i, k: (i, 0)),
                  pl.BlockSpec((te, d), lambda i, k: (k, 0)),
                  pl.BlockSpec((te, d), lambda i, k: (k, 0)),
                  fac, fac, pl.BlockSpec((heads, 1, tm), lambda i, k: (0, 0, i))],
        out_specs=pl.BlockSpec((tm, d), lambda i, k: (i, 0)),
        out_shape=jax.ShapeDtypeStruct((t, d), F32),
        scratch_shapes=[pltpu.VMEM((te, tm), BF16)],
        compiler_params=_params("parallel", "arbitrary"),
    )(h2, u, v, e1, e2, pth)


def kernel(x, mem, norm_mix, w_in, w_a_up, b_a, gla_norm, conv_w, w_br_gla, w_br_conv, w_mem_kv,
           w_br_xa, b_gate, w_o, mem_norm, norm_ffn, peer_wq, peer_subkeys, peer_u, peer_v,
           final_norm):
    batch, seq, d = x.shape
    n_mem = mem.shape[1]
    depth = w_in.shape[0]
    rank, qk = w_a_up.shape[1], w_a_up.shape[2]
    vw = w_br_gla.shape[1]
    cw = w_br_conv.shape[1]
    xw = w_br_xa.shape[1]
    t = batch * seq

    a_off = 2 * qk + 2 * vw
    cols_a = {"q": 0, "k": qk, "v": 2 * qk, "r": 2 * qk + vw}
    cols_b = {"cb": 0, "cc": cw, "ch": 2 * cw}
    xq_col = 3 * cw
    gate_col = xq_col + xw
    assert rank <= LANES and a_off + rank + gate_col + 3 * d == w_in.shape[2]

    xs = x.reshape(t, d)
    mem_n = _rmsnorm(mem.reshape(batch * n_mem, d), mem_norm, BF16)

    for l in range(depth):
        wup = jnp.pad(w_a_up[l], ((0, LANES - rank), (0, 0))).astype(BF16)

        hn = _rmsnorm(xs, norm_mix[l], BF16)
        w_t = jnp.swapaxes(w_in[l], 0, 1)
        proj_a = _mm(hn, _cast_rows(w_t, 0, a_off), BF16, w_is_nk=True)
        proj_b = _mm(hn, _cast_rows(w_t, a_off + rank, gate_col + 3 * d), BF16, w_is_nk=True)
        a_lr = _mm(hn, _cast_rows(w_t, a_off, LANES), BF16, w_is_nk=True)

        o_gla = _gla(proj_a, a_lr, wup, b_a[l].reshape(1, qk), gla_norm[l].reshape(1, -1),
                     batch=batch, seq=seq, qk=qk, vw=vw, cols=cols_a)
        o_conv = _conv(proj_b, conv_w[l], batch=batch, seq=seq, width=cw, cols=cols_b)
        mem_kv = _mm(mem_n, w_mem_kv[l].astype(BF16), BF16)
        o_xa = _xattn(proj_b, mem_kv, batch=batch, seq=seq, n_mem=n_mem, width=xw, col=xq_col)

        merged = _merge(o_gla, o_conv, o_xa, w_br_gla[l].astype(BF16), w_br_conv[l].astype(BF16),
                        w_br_xa[l].astype(BF16), proj_b, b_gate[l].reshape(3, d), d=d,
                        gate_col=gate_col)
        xs = _mm(merged, w_o[l].astype(BF16), F32, residual=xs)

        h2 = _rmsnorm(xs, norm_ffn[l], BF16)
        q2 = _mm(h2, peer_wq[l].astype(BF16), BF16)
        e1, e2, pth = _route(q2, peer_subkeys[l].astype(BF16))
        y = _peer_mix(h2, peer_u[l].astype(BF16), peer_v[l].astype(BF16), e1, e2, pth)
        if l + 1 < depth:
            xs = xs + y
    return _rmsnorm(xs, final_norm, F32, y=y).reshape(batch, seq, d)
```

```python
import functools
import math

import jax
import jax.numpy as jnp
from jax import lax
from jax.experimental import pallas as pl
from jax.experimental.pallas import tpu as pltpu

F32 = jnp.float32
BF16 = jnp.bfloat16
F8 = jnp.float8_e4m3fn
F8_TARGET = 224.0
EPS = 1e-6

GLA_HEADS = 4
GLA_CHUNK = 16
GLA_TAU = 16.0
XA_HEADS = 4
PEER_TOPK = 16

LANES = 128
SUBLANES = 8
VMEM_LIMIT_BYTES = 56 * 1024 * 1024

_NN = (((1,), (0,)), ((), ()))
_NT = (((1,), (1,)), ((), ()))
_TN = (((0,), (0,)), ((), ()))


def _params(*sem):
    return pltpu.CompilerParams(dimension_semantics=sem, vmem_limit_bytes=VMEM_LIMIT_BYTES)


def _tile(n, pref):
    t = min(n, pref)
    while n % t:
        t -= 1
    return t


def _rmsnorm_kernel(x_ref, g_ref, o_ref):
    x = x_ref[...]
    ms = jnp.mean(x * x, axis=-1, keepdims=True)
    o_ref[...] = (x * lax.rsqrt(ms + EPS) * g_ref[...]).astype(o_ref.dtype)


def _add_rmsnorm_kernel(x_ref, y_ref, g_ref, o_ref):
    x = x_ref[...] + y_ref[...]
    ms = jnp.mean(x * x, axis=-1, keepdims=True)
    o_ref[...] = (x * lax.rsqrt(ms + EPS) * g_ref[...]).astype(o_ref.dtype)


def _row_quant(y):
    amax = jnp.max(jnp.abs(y), axis=-1, keepdims=True)
    live = amax > 0.0
    scale = jnp.where(live, F8_TARGET / amax, 1.0)
    inv = jnp.where(live, amax * (1.0 / F8_TARGET), 1.0)
    return (y * scale).astype(F8), inv


def _rmsnorm_q8_kernel(x_ref, g_ref, o_ref, o8_ref, inv_ref):
    x = x_ref[...]
    ms = jnp.mean(x * x, axis=-1, keepdims=True)
    y = x * lax.rsqrt(ms + EPS) * g_ref[...]
    o_ref[...] = y.astype(o_ref.dtype)
    o8_ref[...], inv_ref[...] = _row_quant(y)


def _rmsnorm_q8(x, g):
    m, d = x.shape
    tm = _tile(m, 256)
    row = pl.BlockSpec((tm, d), lambda i: (i, 0))
    return pl.pallas_call(
        _rmsnorm_q8_kernel, grid=(m // tm,),
        in_specs=[row, pl.BlockSpec((1, d), lambda i: (0, 0))],
        out_specs=[row, row, pl.BlockSpec((tm, 1), lambda i: (i, 0))],
        out_shape=[jax.ShapeDtypeStruct((m, d), BF16), jax.ShapeDtypeStruct((m, d), F8),
                   jax.ShapeDtypeStruct((m, 1), F32)],
        compiler_params=_params("parallel"))(x, g.reshape(1, d).astype(F32))


def _quant_rows_kernel(x_ref, o8_ref, inv_ref):
    o8_ref[...], inv_ref[...] = _row_quant(x_ref[...])


def _quant_rows(x):
    m, d = x.shape
    tm = _tile(m, 256)
    row = pl.BlockSpec((tm, d), lambda i: (i, 0))
    return pl.pallas_call(
        _quant_rows_kernel, grid=(m // tm,), in_specs=[row],
        out_specs=[row, pl.BlockSpec((tm, 1), lambda i: (i, 0))],
        out_shape=[jax.ShapeDtypeStruct((m, d), F8), jax.ShapeDtypeStruct((m, 1), F32)],
        compiler_params=_params("parallel"))(x)


def _rmsnorm(x, g, out_dtype, y=None):
    m, d = x.shape
    tm = _tile(m, 256)
    row = pl.BlockSpec((tm, d), lambda i: (i, 0))
    gain = pl.BlockSpec((1, d), lambda i: (0, 0))
    g2 = g.reshape(1, d).astype(F32)
    if y is None:
        body, specs, args = _rmsnorm_kernel, [row, gain], (x, g2)
    else:
        body, specs, args = _add_rmsnorm_kernel, [row, row, gain], (x, y, g2)
    return pl.pallas_call(
        body, grid=(m // tm,), in_specs=specs, out_specs=row,
        out_shape=jax.ShapeDtypeStruct((m, d), out_dtype),
        compiler_params=_params("parallel"))(*args)


def _mm_kernel(x_ref, w_ref, o_ref, *, dims):
    o_ref[...] = lax.dot_general(x_ref[...], w_ref[...], dims,
                                 preferred_element_type=F32).astype(o_ref.dtype)


def _mm_res_kernel(x_ref, w_ref, r_ref, o_ref, *, dims):
    acc = lax.dot_general(x_ref[...], w_ref[...], dims, preferred_element_type=F32)
    o_ref[...] = (r_ref[...] + acc).astype(o_ref.dtype)


def _mm(x, w, out_dtype, residual=None, w_is_nk=False, tm=1024, tn=1024):
    m, k = x.shape
    n = w.shape[0] if w_is_nk else w.shape[1]
    tm, tn = _tile(m, tm), _tile(n, tn)
    xs = pl.BlockSpec((tm, k), lambda i, j: (i, 0))
    if w_is_nk:
        ws, dims = pl.BlockSpec((tn, k), lambda i, j: (j, 0)), _NT
    else:
        ws, dims = pl.BlockSpec((k, tn), lambda i, j: (0, j)), _NN
    os_ = pl.BlockSpec((tm, tn), lambda i, j: (i, j))
    if residual is None:
        body, specs, args = _mm_kernel, [xs, ws], (x, w)
    else:
        body, specs, args = _mm_res_kernel, [xs, ws, os_], (x, w, residual)
    return pl.pallas_call(
        functools.partial(body, dims=dims), grid=(m // tm, n // tn), in_specs=specs,
        out_specs=os_, out_shape=jax.ShapeDtypeStruct((m, n), out_dtype),
        compiler_params=_params("parallel", "parallel"))(*args)


def _cast_rows_kernel(*refs, shift):
    o_ref = refs[-1]
    if shift:
        a_ref, b_ref = refs[0], refs[1]
        o_ref[...] = jnp.concatenate([a_ref[shift:, :], b_ref[:shift, :]],
                                     axis=0).astype(o_ref.dtype)
    else:
        o_ref[...] = refs[0][...].astype(o_ref.dtype)


def _cast_rows(w, start, rows):
    k = w.shape[1]
    tr = _tile(rows, 256)
    shift = start % tr
    assert shift % SUBLANES == 0
    first = start // tr
    specs = [pl.BlockSpec((tr, k), lambda i: (first + i, 0))]
    if shift:
        specs.append(pl.BlockSpec((tr, k), lambda i: (first + i + 1, 0)))
    return pl.pallas_call(
        functools.partial(_cast_rows_kernel, shift=shift), grid=(rows // tr,), in_specs=specs,
        out_specs=pl.BlockSpec((tr, k), lambda i: (i, 0)),
        out_shape=jax.ShapeDtypeStruct((rows, k), BF16),
        compiler_params=_params("parallel"))(*([w] * len(specs)))


def _gla_kernel(q_ref, k_ref, v_ref, r_ref, a_ref, wup_ref, ba_ref, gn_ref, o_ref, b_scr, s_scr,
                *, heads, chunk, tau):
    @pl.when(pl.program_id(1) == 0)
    def _():
        s_scr[...] = jnp.zeros_like(s_scr)

    blk = q_ref.shape[0]
    dk = q_ref.shape[1] // heads
    dv = v_ref.shape[1] // heads
    shift = chunk.bit_length() - 1

    z = jnp.dot(a_ref[...], wup_ref[...], preferred_element_type=F32) + ba_ref[...]
    log_a = -(jnp.maximum(-z, 0.0) + jnp.log1p(jnp.exp(-jnp.abs(z)))) * (1.0 / tau)
    row = lax.broadcasted_iota(jnp.int32, (blk, blk), 0)
    col = lax.broadcasted_iota(jnp.int32, (blk, blk), 1)
    same_chunk = jnp.right_shift(row, shift) == jnp.right_shift(col, shift)
    cum = jnp.where(same_chunk & (col <= row), 1.0, 0.0).astype(BF16)
    hi = log_a.astype(BF16)
    lo = (log_a - hi.astype(F32)).astype(BF16)
    b_scr[...] = (jnp.dot(cum, hi, preferred_element_type=F32)
                  + jnp.dot(cum, lo, preferred_element_type=F32))

    pair = 2 * chunk
    trow = lax.broadcasted_iota(jnp.int32, (pair, pair), 0)
    tcol = lax.broadcasted_iota(jnp.int32, (pair, pair), 1)
    intra = ((trow < chunk) == (tcol < chunk)) & (tcol <= trow)
    cross = (trow >= chunk) & (tcol < chunk)
    in_a = lax.broadcasted_iota(jnp.int32, (pair, 1), 0) < chunk
    gn = gn_ref[...]

    def body(c, carry):
        rows = pl.ds(pl.multiple_of(c * pair, pair), pair)
        for h in range(heads):
            kc = slice(h * dk, (h + 1) * dk)
            vc = slice(h * dv, (h + 1) * dv)
            b = b_scr[rows, kc]
            last_a = b[chunk - 1:chunk, :]
            last_b = b[pair - 1:pair, :]
            dec_a = jnp.exp(last_a)
            dec_b = jnp.exp(last_b)
            q = q_ref[rows, kc].astype(F32)
            k = k_ref[rows, kc].astype(F32)
            v = v_ref[rows, vc]
            q_t = q * jnp.exp(b) * (dk ** -0.5)
            k_in = (k * jnp.exp(-b)).astype(BF16)
            k_out = k * jnp.exp(jnp.where(in_a, last_a, last_b) - b)
            q_b = q_t.astype(BF16)
            att_in = lax.dot_general(q_b, k_in, _NT, preferred_element_type=F32)
            att_x = lax.dot_general(q_b, k_out.astype(BF16), _NT, preferred_element_type=F32)
            att = jnp.where(intra, att_in, jnp.where(cross, att_x, 0.0)).astype(BF16)
            st = s_scr[h]
            q_s = (q_t * jnp.where(in_a, 1.0, dec_a)).astype(BF16)
            o = (jnp.dot(att, v, preferred_element_type=F32)
                 + lax.dot_general(q_s, st.astype(BF16), _NT, preferred_element_type=F32))
            k_s = (k_out * jnp.where(in_a, dec_b, 1.0)).astype(BF16)
            s_scr[h] = st * (dec_a * dec_b) + lax.dot_general(
                v, k_s, _TN, preferred_element_type=F32)
            ms = jnp.mean(o * o, axis=-1, keepdims=True)
            y = o * lax.rsqrt(ms + EPS) * gn
            r = r_ref[rows, vc].astype(F32)
            o_ref[rows, vc] = (y * (r * jax.nn.sigmoid(r))).astype(o_ref.dtype)
        return carry

    lax.fori_loop(0, blk // pair, body, 0)


def _gla(proj, a_lr, wup, ba, gn, *, batch, seq, qk, vw, cols):
    blk = _tile(seq, 256)
    nb = seq // blk
    dv = vw // GLA_HEADS

    def at(width, off):
        return pl.BlockSpec((blk, width), lambda b, j: (b * nb + j, off // width))

    full = lambda shape: pl.BlockSpec(shape, lambda b, j: (0,) * len(shape))
    return pl.pallas_call(
        functools.partial(_gla_kernel, heads=GLA_HEADS, chunk=GLA_CHUNK, tau=GLA_TAU),
        grid=(batch, nb),
        in_specs=[at(qk, cols["q"]), at(qk, cols["k"]), at(vw, cols["v"]), at(vw, cols["r"]),
                  pl.BlockSpec((blk, a_lr.shape[1]), lambda b, j: (b * nb + j, 0)),
                  full(wup.shape), full(ba.shape), full(gn.shape)],
        out_specs=pl.BlockSpec((blk, vw), lambda b, j: (b * nb + j, 0)),
        out_shape=jax.ShapeDtypeStruct((batch * seq, vw), BF16),
        scratch_shapes=[pltpu.VMEM((blk, qk), F32),
                        pltpu.VMEM((GLA_HEADS, dv, qk // GLA_HEADS), F32)],
        compiler_params=_params("parallel", "arbitrary"),
    )(proj, proj, proj, proj, a_lr, wup, ba, gn)


def _conv_kernel(cb_ref, cc_ref, ch_ref, pc_ref, ph_ref, w_ref, o_ref, u_scr):
    ts = cc_ref.shape[0]
    halo = pc_ref.shape[0]
    u = cc_ref[...].astype(F32) * ch_ref[...].astype(F32)
    prev = pc_ref[...].astype(F32) * ph_ref[...].astype(F32)
    u_scr[0:halo, :] = jnp.where(pl.program_id(1) > 0, prev, 0.0)
    u_scr[halo:halo + ts, :] = u
    w = w_ref[...]
    y = (w[0:1] * u + w[1:2] * u_scr[halo - 1:halo - 1 + ts, :]
         + w[2:3] * u_scr[halo - 2:halo - 2 + ts, :])
    o_ref[...] = (cb_ref[...].astype(F32) * y).astype(o_ref.dtype)


def _conv(proj, conv_w, *, batch, seq, width, cols):
    ts = _tile(seq, 512)
    tw = _tile(width, 1024)
    halo = 16
    ns, nw = seq // ts, width // tw

    def at(off):
        return pl.BlockSpec((ts, tw), lambda b, j, c: (b * ns + j, off // tw + c))

    def halo_at(off):
        return pl.BlockSpec(
            (halo, tw),
            lambda b, j, c: (jnp.maximum((b * seq + j * ts) // halo - 1, 0), off // tw + c))

    return pl.pallas_call(
        _conv_kernel, grid=(batch, ns, nw),
        in_specs=[at(cols["cb"]), at(cols["cc"]), at(cols["ch"]),
                  halo_at(cols["cc"]), halo_at(cols["ch"]),
                  pl.BlockSpec((conv_w.shape[0], tw), lambda b, j, c: (0, c))],
        out_specs=pl.BlockSpec((ts, tw), lambda b, j, c: (b * ns + j, c)),
        out_shape=jax.ShapeDtypeStruct((batch * seq, width), BF16),
        scratch_shapes=[pltpu.VMEM((halo + ts, tw), F32)],
        compiler_params=_params("parallel", "parallel", "parallel"),
    )(proj, proj, proj, proj, proj, conv_w)


def _xa_kernel(q_ref, k_ref, v_ref, o_ref, *, heads):
    dh = q_ref.shape[1] // heads
    for h in range(heads):
        cs = slice(h * dh, (h + 1) * dh)
        s = lax.dot_general(q_ref[:, cs], k_ref[:, cs], _NT,
                            preferred_element_type=F32) * (dh ** -0.5)
        p = jnp.exp(s - jnp.max(s, axis=-1, keepdims=True))
        l = jnp.sum(p, axis=-1, keepdims=True)
        o = jnp.dot(p.astype(BF16), v_ref[:, cs], preferred_element_type=F32)
        o_ref[:, cs] = (o / l).astype(o_ref.dtype)


def _xattn(proj, mem_kv, *, batch, seq, n_mem, width, col):
    tq = _tile(seq, 512)
    nq = seq // tq
    return pl.pallas_call(
        functools.partial(_xa_kernel, heads=XA_HEADS), grid=(batch, nq),
        in_specs=[pl.BlockSpec((tq, width), lambda b, j: (b * nq + j, col // width)),
                  pl.BlockSpec((n_mem, width), lambda b, j: (b, 0)),
                  pl.BlockSpec((n_mem, width), lambda b, j: (b, 1))],
        out_specs=pl.BlockSpec((tq, width), lambda b, j: (b * nq + j, 0)),
        out_shape=jax.ShapeDtypeStruct((batch * seq, width), BF16),
        compiler_params=_params("parallel", "parallel"),
    )(proj, mem_kv, mem_kv)


def _merge_kernel(og_ref, oc_ref, ox_ref, wg_ref, wc_ref, wx_ref, g0_ref, g1_ref, g2_ref,
                  bg_ref, o_ref):
    acc = None
    for idx, (a_ref, w_ref, g_ref) in enumerate(
            ((og_ref, wg_ref, g0_ref), (oc_ref, wc_ref, g1_ref), (ox_ref, wx_ref, g2_ref))):
        y = jnp.dot(a_ref[...], w_ref[...], preferred_element_type=F32)
        g = jax.nn.sigmoid(g_ref[...].astype(F32) + bg_ref[idx:idx + 1, :])
        acc = g * y if acc is None else acc + g * y
    o_ref[...] = acc.astype(o_ref.dtype)


def _merge(o_gla, o_conv, o_xa, w_gla, w_conv, w_xa, proj, b_gate, *, d, gate_col):
    m = o_gla.shape[0]
    tm, tn = _tile(m, 512), _tile(d, 1024)

    def lhs(a):
        return pl.BlockSpec((tm, a.shape[1]), lambda i, j: (i, 0))

    def rhs(w):
        return pl.BlockSpec((w.shape[0], tn), lambda i, j: (0, j))

    def gate(g):
        return pl.BlockSpec((tm, tn), lambda i, j: (i, (gate_col + g * d) // tn + j))

    return pl.pallas_call(
        _merge_kernel, grid=(m // tm, d // tn),
        in_specs=[lhs(o_gla), lhs(o_conv), lhs(o_xa), rhs(w_gla), rhs(w_conv), rhs(w_xa),
                  gate(0), gate(1), gate(2),
                  pl.BlockSpec((b_gate.shape[0], tn), lambda i, j: (0, j))],
        out_specs=pl.BlockSpec((tm, tn), lambda i, j: (i, j)),
        out_shape=jax.ShapeDtypeStruct((m, d), BF16),
        compiler_params=_params("parallel", "parallel"),
    )(o_gla, o_conv, o_xa, w_gla, w_conv, w_xa, proj, proj, proj, b_gate)


def _n_candidates(topk):
    return sum(topk // p for p in range(1, topk + 1))


def _route_kernel(q_ref, sk_ref, e1_ref, e2_ref, pth_ref, a_scr, b_scr, c_scr, *, topk):
    dq = sk_ref.shape[3]

    def half(c, top_scr):
        s = lax.dot_general(sk_ref[0, c], q_ref[:, c * dq:(c + 1) * dq], _NT,
                            preferred_element_type=F32)
        e = jnp.exp(s - jnp.max(s, axis=0, keepdims=True))
        top = jnp.max(e, axis=0, keepdims=True)
        top_scr[0:1, :] = top
        for r in range(1, topk):
            top = jnp.max(jnp.where(e < top, e, -1.0), axis=0, keepdims=True)
            top_scr[r:r + 1, :] = jnp.maximum(top, 0.0)
        return e

    e1 = half(0, a_scr)
    e2 = half(1, b_scr)

    def candidates(scale):
        off = 0
        for p in range(topk):
            nq = topk // (p + 1)
            c_scr[off:off + nq, :] = a_scr[p:p + 1, :] * (b_scr[0:nq, :] * scale)
            off += nq
        if off < c_scr.shape[0]:
            c_scr[off:, :] = jnp.full((c_scr.shape[0] - off, c_scr.shape[1]), -1.0, F32)
        return c_scr[...]

    cand = candidates(1.0)
    kth = jnp.max(cand, axis=0, keepdims=True)
    for r in range(1, topk):
        kth = jnp.max(jnp.where(cand < kth, cand, -1.0), axis=0, keepdims=True)
    kth = jnp.maximum(kth, 0.0)
    sel = cand >= kth
    inv_z = 1.0 / jnp.sum(jnp.where(sel, cand, 0.0), axis=0, keepdims=True)
    cand_n = candidates(inv_z)
    e1_ref[0] = e1
    e2_ref[0] = e2 * inv_z
    pth_ref[0] = jnp.min(jnp.where(sel, cand_n, jnp.inf), axis=0, keepdims=True)


def _route(q2, sub_keys):
    t = q2.shape[0]
    heads, _, nkeys, dq = sub_keys.shape
    tm = _tile(t, 512)
    ncand = -(-_n_candidates(PEER_TOPK) // 8) * 8
    fac = pl.BlockSpec((1, nkeys, tm), lambda i, h: (h, 0, i))
    return pl.pallas_call(
        functools.partial(_route_kernel, topk=PEER_TOPK), grid=(t // tm, heads),
        in_specs=[pl.BlockSpec((tm, 2 * dq), lambda i, h: (i, h)),
                  pl.BlockSpec((1, 2, nkeys, dq), lambda i, h: (h, 0, 0, 0))],
        out_specs=[fac, fac, pl.BlockSpec((1, 1, tm), lambda i, h: (h, 0, i))],
        out_shape=[jax.ShapeDtypeStruct((heads, nkeys, t), F32),
                   jax.ShapeDtypeStruct((heads, nkeys, t), F32),
                   jax.ShapeDtypeStruct((heads, 1, t), F32)],
        scratch_shapes=[pltpu.VMEM((PEER_TOPK, tm), F32), pltpu.VMEM((PEER_TOPK, tm), F32),
                        pltpu.VMEM((ncand, tm), F32)],
        compiler_params=_params("parallel", "parallel"),
    )(q2, sub_keys)


def _peer_kernel(h_ref, hs_ref, u_ref, us_ref, v_ref, e1_ref, e2_ref, pth_ref, o_ref, w_scr,
                 *, heads, nkeys):
    kstep = pl.program_id(1)
    tm = h_ref.shape[0]
    te = u_ref.shape[0]
    ni = te // nkeys

    @pl.when(kstep == 0)
    def _():
        o_ref[...] = jnp.zeros_like(o_ref)

    parts = SUBLANES // ni
    group = pl.ds(pl.multiple_of((kstep // parts) * SUBLANES, SUBLANES), SUBLANES)
    part = kstep % parts

    def e1_row(h, ii, ls):
        grp = e1_ref[h, group, ls]
        row = grp[ii:ii + 1]
        for s in range(1, parts):
            row = jnp.where(part == s, grp[s * ni + ii:s * ni + ii + 1], row)
        return row

    a_t = lax.dot_general(u_ref[...], h_ref[...], _NT, preferred_element_type=F32)
    for ii in range(ni):
        rs = slice(ii * nkeys, (ii + 1) * nkeys)
        u_scale = jnp.broadcast_to(us_ref[rs, :], (nkeys, LANES))
        for tc in range(tm // LANES):
            ls = slice(tc * LANES, (tc + 1) * LANES)
            g = jnp.zeros((nkeys, LANES), F32)
            for h in range(heads):
                p = e1_row(h, ii, ls) * e2_ref[h, :, ls]
                g = g + jnp.where(p >= pth_ref[h, :, ls], p, 0.0)
            a = a_t[rs, ls] * (u_scale * hs_ref[:, ls])
            gelu = 0.5 * a * (1.0 + lax.erf(a * (1.0 / math.sqrt(2.0))))
            w_scr[rs, ls] = (g * gelu).astype(BF16)
    o_ref[...] += lax.dot_general(w_scr[...], v_ref[...], _TN, preferred_element_type=F32)


def _peer_mix(h8, h_scale, u8, u_scale, v, e1, e2, pth):
    t, d = h8.shape
    n_exp = u8.shape[0]
    heads, nkeys, _ = e1.shape
    tm, te = _tile(t, 512), _tile(n_exp, 512)
    fac = pl.BlockSpec((heads, nkeys, tm), lambda i, k: (0, 0, i))
    return pl.pallas_call(
        functools.partial(_peer_kernel, heads=heads, nkeys=nkeys),
        grid=(t // tm, n_exp // te),
        in_specs=[pl.BlockSpec((tm, d), lambda i, k: (i, 0)),
                  pl.BlockSpec((1, tm), lambda i, k: (0, i)),
                  pl.BlockSpec((te, d), lambda i, k: (k, 0)),
                  pl.BlockSpec((te, 1), lambda i, k: (k, 0)),
                  pl.BlockSpec((te, d), lambda i, k: (k, 0)),
                  fac, fac, pl.BlockSpec((heads, 1, tm), lambda i, k: (0, 0, i))],
        out_specs=pl.BlockSpec((tm, d), lambda i, k: (i, 0)),
        out_shape=jax.ShapeDtypeStruct((t, d), F32),
        scratch_shapes=[pltpu.VMEM((te, tm), BF16)],
        compiler_params=_params("parallel", "arbitrary"),
    )(h8, h_scale, u8, u_scale, v, e1, e2, pth)


def kernel(x, mem, norm_mix, w_in, w_a_up, b_a, gla_norm, conv_w, w_br_gla, w_br_conv, w_mem_kv,
           w_br_xa, b_gate, w_o, mem_norm, norm_ffn, peer_wq, peer_subkeys, peer_u, peer_v,
           final_norm):
    batch, seq, d = x.shape
    n_mem = mem.shape[1]
    depth = w_in.shape[0]
    rank, qk = w_a_up.shape[1], w_a_up.shape[2]
    vw = w_br_gla.shape[1]
    cw = w_br_conv.shape[1]
    xw = w_br_xa.shape[1]
    t = batch * seq

    a_off = 2 * qk + 2 * vw
    cols_a = {"q": 0, "k": qk, "v": 2 * qk, "r": 2 * qk + vw}
    cols_b = {"cb": 0, "cc": cw, "ch": 2 * cw}
    xq_col = 3 * cw
    gate_col = xq_col + xw
    assert rank <= LANES and a_off + rank + gate_col + 3 * d == w_in.shape[2]

    xs = x.reshape(t, d)
    mem_n = _rmsnorm(mem.reshape(batch * n_mem, d), mem_norm, BF16)

    for l in range(depth):
        wup = jnp.pad(w_a_up[l], ((0, LANES - rank), (0, 0))).astype(BF16)

        hn = _rmsnorm(xs, norm_mix[l], BF16)
        w_t = jnp.swapaxes(w_in[l], 0, 1)
        proj_a = _mm(hn, _cast_rows(w_t, 0, a_off), BF16, w_is_nk=True)
        proj_b = _mm(hn, _cast_rows(w_t, a_off + rank, gate_col + 3 * d), BF16, w_is_nk=True)
        a_lr = _mm(hn, _cast_rows(w_t, a_off, LANES), BF16, w_is_nk=True)

        o_gla = _gla(proj_a, a_lr, wup, b_a[l].reshape(1, qk), gla_norm[l].reshape(1, -1),
                     batch=batch, seq=seq, qk=qk, vw=vw, cols=cols_a)
        o_conv = _conv(proj_b, conv_w[l], batch=batch, seq=seq, width=cw, cols=cols_b)
        mem_kv = _mm(mem_n, w_mem_kv[l].astype(BF16), BF16)
        o_xa = _xattn(proj_b, mem_kv, batch=batch, seq=seq, n_mem=n_mem, width=xw, col=xq_col)

        merged = _merge(o_gla, o_conv, o_xa, w_br_gla[l].astype(BF16), w_br_conv[l].astype(BF16),
                        w_br_xa[l].astype(BF16), proj_b, b_gate[l].reshape(3, d), d=d,
                        gate_col=gate_col)
        xs = _mm(merged, w_o[l].astype(BF16), F32, residual=xs)

        h2, h8, h_scale = _rmsnorm_q8(xs, norm_ffn[l])
        u8, u_scale = _quant_rows(peer_u[l])
        q2 = _mm(h2, peer_wq[l].astype(BF16), BF16)
        e1, e2, pth = _route(q2, peer_subkeys[l].astype(BF16))
        y = _peer_mix(h8, h_scale.reshape(1, t), u8, u_scale, peer_v[l].astype(BF16), e1, e2, pth)
        if l + 1 < depth:
            xs = xs + y
    return _rmsnorm(xs, final_norm, F32, y=y).reshape(batch, seq, d)
```

```python
import functools
import math

import jax
import jax.numpy as jnp
from jax import lax
from jax.experimental import pallas as pl
from jax.experimental.pallas import tpu as pltpu

F32 = jnp.float32
BF16 = jnp.bfloat16
E4M3 = jnp.float8_e4m3fn
E5M2 = jnp.float8_e5m2
E4M3_TARGET = 224.0
E5M2_TARGET = 16384.0
EPS = 1e-6

GLA_HEADS = 4
GLA_CHUNK = 16
GLA_TAU = 16.0
XA_HEADS = 4
PEER_TOPK = 16

LANES = 128
SUBLANES = 8
VMEM_LIMIT_BYTES = 56 * 1024 * 1024

_NN = (((1,), (0,)), ((), ()))
_NT = (((1,), (1,)), ((), ()))
_TN = (((0,), (0,)), ((), ()))


def _params(*sem):
    return pltpu.CompilerParams(dimension_semantics=sem, vmem_limit_bytes=VMEM_LIMIT_BYTES)


def _tile(n, pref):
    t = min(n, pref)
    while n % t:
        t -= 1
    return t


def _rmsnorm_kernel(x_ref, g_ref, o_ref):
    x = x_ref[...]
    ms = jnp.mean(x * x, axis=-1, keepdims=True)
    o_ref[...] = (x * lax.rsqrt(ms + EPS) * g_ref[...]).astype(o_ref.dtype)


def _add_rmsnorm_kernel(x_ref, y_ref, c_ref, g_ref, o_ref):
    x = x_ref[...] + y_ref[...] * c_ref[...]
    ms = jnp.mean(x * x, axis=-1, keepdims=True)
    o_ref[...] = (x * lax.rsqrt(ms + EPS) * g_ref[...]).astype(o_ref.dtype)


def _row_quant(y, dtype, target):
    amax = jnp.max(jnp.abs(y), axis=-1, keepdims=True)
    live = amax > 0.0
    scale = jnp.where(live, target / amax, 1.0)
    inv = jnp.where(live, amax * (1.0 / target), 1.0)
    return (y * scale).astype(dtype), inv


def _rmsnorm_q8_kernel(x_ref, g_ref, o_ref, o8_ref, inv_ref):
    x = x_ref[...]
    ms = jnp.mean(x * x, axis=-1, keepdims=True)
    y = x * lax.rsqrt(ms + EPS) * g_ref[...]
    o_ref[...] = y.astype(o_ref.dtype)
    o8_ref[...], inv_ref[...] = _row_quant(y, o8_ref.dtype, E4M3_TARGET)


def _rmsnorm_q8(x, g):
    m, d = x.shape
    tm = _tile(m, 256)
    row = pl.BlockSpec((tm, d), lambda i: (i, 0))
    return pl.pallas_call(
        _rmsnorm_q8_kernel, grid=(m // tm,),
        in_specs=[row, pl.BlockSpec((1, d), lambda i: (0, 0))],
        out_specs=[row, row, pl.BlockSpec((tm, 1), lambda i: (i, 0))],
        out_shape=[jax.ShapeDtypeStruct((m, d), BF16), jax.ShapeDtypeStruct((m, d), E4M3),
                   jax.ShapeDtypeStruct((m, 1), F32)],
        compiler_params=_params("parallel"))(x, g.reshape(1, d).astype(F32))


def _quant_rows_kernel(x_ref, o8_ref, inv_ref, *, target):
    o8_ref[...], inv_ref[...] = _row_quant(x_ref[...], o8_ref.dtype, target)


def _quant_rows(x, dtype, target):
    m, d = x.shape
    tm = _tile(m, 256)
    row = pl.BlockSpec((tm, d), lambda i: (i, 0))
    return pl.pallas_call(
        functools.partial(_quant_rows_kernel, target=target), grid=(m // tm,), in_specs=[row],
        out_specs=[row, pl.BlockSpec((tm, 1), lambda i: (i, 0))],
        out_shape=[jax.ShapeDtypeStruct((m, d), dtype), jax.ShapeDtypeStruct((m, 1), F32)],
        compiler_params=_params("parallel"))(x)


def _rmsnorm(x, g, out_dtype, y=None, y_scale=None):
    m, d = x.shape
    tm = _tile(m, 256)
    row = pl.BlockSpec((tm, d), lambda i: (i, 0))
    one = lambda w: pl.BlockSpec((1, w), lambda i: (0, 0))
    g2 = g.reshape(1, d).astype(F32)
    if y is None:
        body, specs, args = _rmsnorm_kernel, [row, one(d)], (x, g2)
    else:
        c = jnp.ones((1, 1), F32) if y_scale is None else y_scale.reshape(1, 1).astype(F32)
        body, specs, args = _add_rmsnorm_kernel, [row, row, one(1), one(d)], (x, y, c, g2)
    return pl.pallas_call(
        body, grid=(m // tm,), in_specs=specs, out_specs=row,
        out_shape=jax.ShapeDtypeStruct((m, d), out_dtype),
        compiler_params=_params("parallel"))(*args)


def _mm_kernel(x_ref, w_ref, o_ref, *, dims):
    o_ref[...] = lax.dot_general(x_ref[...], w_ref[...], dims,
                                 preferred_element_type=F32).astype(o_ref.dtype)


def _mm_res_kernel(x_ref, w_ref, r_ref, o_ref, *, dims):
    acc = lax.dot_general(x_ref[...], w_ref[...], dims, preferred_element_type=F32)
    o_ref[...] = (r_ref[...] + acc).astype(o_ref.dtype)


def _mm(x, w, out_dtype, residual=None, w_is_nk=False, tm=1024, tn=1024):
    m, k = x.shape
    n = w.shape[0] if w_is_nk else w.shape[1]
    tm, tn = _tile(m, tm), _tile(n, tn)
    xs = pl.BlockSpec((tm, k), lambda i, j: (i, 0))
    if w_is_nk:
        ws, dims = pl.BlockSpec((tn, k), lambda i, j: (j, 0)), _NT
    else:
        ws, dims = pl.BlockSpec((k, tn), lambda i, j: (0, j)), _NN
    os_ = pl.BlockSpec((tm, tn), lambda i, j: (i, j))
    if residual is None:
        body, specs, args = _mm_kernel, [xs, ws], (x, w)
    else:
        body, specs, args = _mm_res_kernel, [xs, ws, os_], (x, w, residual)
    return pl.pallas_call(
        functools.partial(body, dims=dims), grid=(m // tm, n // tn), in_specs=specs,
        out_specs=os_, out_shape=jax.ShapeDtypeStruct((m, n), out_dtype),
        compiler_params=_params("parallel", "parallel"))(*args)


def _cast_rows_kernel(*refs, shift):
    o_ref = refs[-1]
    if shift:
        a_ref, b_ref = refs[0], refs[1]
        o_ref[...] = jnp.concatenate([a_ref[shift:, :], b_ref[:shift, :]],
                                     axis=0).astype(o_ref.dtype)
    else:
        o_ref[...] = refs[0][...].astype(o_ref.dtype)


def _cast_rows(w, start, rows):
    k = w.shape[1]
    tr = _tile(rows, 256)
    shift = start % tr
    assert shift % SUBLANES == 0
    first = start // tr
    specs = [pl.BlockSpec((tr, k), lambda i: (first + i, 0))]
    if shift:
        specs.append(pl.BlockSpec((tr, k), lambda i: (first + i + 1, 0)))
    return pl.pallas_call(
        functools.partial(_cast_rows_kernel, shift=shift), grid=(rows // tr,), in_specs=specs,
        out_specs=pl.BlockSpec((tr, k), lambda i: (i, 0)),
        out_shape=jax.ShapeDtypeStruct((rows, k), BF16),
        compiler_params=_params("parallel"))(*([w] * len(specs)))


def _gla_kernel(q_ref, k_ref, v_ref, r_ref, a_ref, wup_ref, ba_ref, gn_ref, o_ref, b_scr, s_scr,
                *, heads, chunk, tau):
    @pl.when(pl.program_id(1) == 0)
    def _():
        s_scr[...] = jnp.zeros_like(s_scr)

    blk = q_ref.shape[0]
    dk = q_ref.shape[1] // heads
    dv = v_ref.shape[1] // heads
    shift = chunk.bit_length() - 1

    z = jnp.dot(a_ref[...], wup_ref[...], preferred_element_type=F32) + ba_ref[...]
    log_a = -(jnp.maximum(-z, 0.0) + jnp.log1p(jnp.exp(-jnp.abs(z)))) * (1.0 / tau)
    row = lax.broadcasted_iota(jnp.int32, (blk, blk), 0)
    col = lax.broadcasted_iota(jnp.int32, (blk, blk), 1)
    same_chunk = jnp.right_shift(row, shift) == jnp.right_shift(col, shift)
    cum = jnp.where(same_chunk & (col <= row), 1.0, 0.0).astype(BF16)
    hi = log_a.astype(BF16)
    lo = (log_a - hi.astype(F32)).astype(BF16)
    b_scr[...] = (jnp.dot(cum, hi, preferred_element_type=F32)
                  + jnp.dot(cum, lo, preferred_element_type=F32))

    pair = 2 * chunk
    trow = lax.broadcasted_iota(jnp.int32, (pair, pair), 0)
    tcol = lax.broadcasted_iota(jnp.int32, (pair, pair), 1)
    intra = ((trow < chunk) == (tcol < chunk)) & (tcol <= trow)
    cross = (trow >= chunk) & (tcol < chunk)
    in_a = lax.broadcasted_iota(jnp.int32, (pair, 1), 0) < chunk
    gn = gn_ref[...]

    def body(c, carry):
        rows = pl.ds(pl.multiple_of(c * pair, pair), pair)
        for h in range(heads):
            kc = slice(h * dk, (h + 1) * dk)
            vc = slice(h * dv, (h + 1) * dv)
            b = b_scr[rows, kc]
            last_a = b[chunk - 1:chunk, :]
            last_b = b[pair - 1:pair, :]
            dec_a = jnp.exp(last_a)
            dec_b = jnp.exp(last_b)
            q = q_ref[rows, kc].astype(F32)
            k = k_ref[rows, kc].astype(F32)
            v = v_ref[rows, vc]
            q_t = q * jnp.exp(b) * (dk ** -0.5)
            k_in = (k * jnp.exp(-b)).astype(BF16)
            k_out = k * jnp.exp(jnp.where(in_a, last_a, last_b) - b)
            q_b = q_t.astype(BF16)
            att_in = lax.dot_general(q_b, k_in, _NT, preferred_element_type=F32)
            att_x = lax.dot_general(q_b, k_out.astype(BF16), _NT, preferred_element_type=F32)
            att = jnp.where(intra, att_in, jnp.where(cross, att_x, 0.0)).astype(BF16)
            st = s_scr[h]
            q_s = (q_t * jnp.where(in_a, 1.0, dec_a)).astype(BF16)
            o = (jnp.dot(att, v, preferred_element_type=F32)
                 + lax.dot_general(q_s, st.astype(BF16), _NT, preferred_element_type=F32))
            k_s = (k_out * jnp.where(in_a, dec_b, 1.0)).astype(BF16)
            s_scr[h] = st * (dec_a * dec_b) + lax.dot_general(
                v, k_s, _TN, preferred_element_type=F32)
            ms = jnp.mean(o * o, axis=-1, keepdims=True)
            y = o * lax.rsqrt(ms + EPS) * gn
            r = r_ref[rows, vc].astype(F32)
            o_ref[rows, vc] = (y * (r * jax.nn.sigmoid(r))).astype(o_ref.dtype)
        return carry

    lax.fori_loop(0, blk // pair, body, 0)


def _gla(proj, a_lr, wup, ba, gn, *, batch, seq, qk, vw, cols):
    blk = _tile(seq, 256)
    nb = seq // blk
    dv = vw // GLA_HEADS

    def at(width, off):
        return pl.BlockSpec((blk, width), lambda b, j: (b * nb + j, off // width))

    full = lambda shape: pl.BlockSpec(shape, lambda b, j: (0,) * len(shape))
    return pl.pallas_call(
        functools.partial(_gla_kernel, heads=GLA_HEADS, chunk=GLA_CHUNK, tau=GLA_TAU),
        grid=(batch, nb),
        in_specs=[at(qk, cols["q"]), at(qk, cols["k"]), at(vw, cols["v"]), at(vw, cols["r"]),
                  pl.BlockSpec((blk, a_lr.shape[1]), lambda b, j: (b * nb + j, 0)),
                  full(wup.shape), full(ba.shape), full(gn.shape)],
        out_specs=pl.BlockSpec((blk, vw), lambda b, j: (b * nb + j, 0)),
        out_shape=jax.ShapeDtypeStruct((batch * seq, vw), BF16),
        scratch_shapes=[pltpu.VMEM((blk, qk), F32),
                        pltpu.VMEM((GLA_HEADS, dv, qk // GLA_HEADS), F32)],
        compiler_params=_params("parallel", "arbitrary"),
    )(proj, proj, proj, proj, a_lr, wup, ba, gn)


def _conv_kernel(cb_ref, cc_ref, ch_ref, pc_ref, ph_ref, w_ref, o_ref, u_scr):
    ts = cc_ref.shape[0]
    halo = pc_ref.shape[0]
    u = cc_ref[...].astype(F32) * ch_ref[...].astype(F32)
    prev = pc_ref[...].astype(F32) * ph_ref[...].astype(F32)
    u_scr[0:halo, :] = jnp.where(pl.program_id(1) > 0, prev, 0.0)
    u_scr[halo:halo + ts, :] = u
    w = w_ref[...]
    y = (w[0:1] * u + w[1:2] * u_scr[halo - 1:halo - 1 + ts, :]
         + w[2:3] * u_scr[halo - 2:halo - 2 + ts, :])
    o_ref[...] = (cb_ref[...].astype(F32) * y).astype(o_ref.dtype)


def _conv(proj, conv_w, *, batch, seq, width, cols):
    ts = _tile(seq, 512)
    tw = _tile(width, 1024)
    halo = 16
    ns, nw = seq // ts, width // tw

    def at(off):
        return pl.BlockSpec((ts, tw), lambda b, j, c: (b * ns + j, off // tw + c))

    def halo_at(off):
        return pl.BlockSpec(
            (halo, tw),
            lambda b, j, c: (jnp.maximum((b * seq + j * ts) // halo - 1, 0), off // tw + c))

    return pl.pallas_call(
        _conv_kernel, grid=(batch, ns, nw),
        in_specs=[at(cols["cb"]), at(cols["cc"]), at(cols["ch"]),
                  halo_at(cols["cc"]), halo_at(cols["ch"]),
                  pl.BlockSpec((conv_w.shape[0], tw), lambda b, j, c: (0, c))],
        out_specs=pl.BlockSpec((ts, tw), lambda b, j, c: (b * ns + j, c)),
        out_shape=jax.ShapeDtypeStruct((batch * seq, width), BF16),
        scratch_shapes=[pltpu.VMEM((halo + ts, tw), F32)],
        compiler_params=_params("parallel", "parallel", "parallel"),
    )(proj, proj, proj, proj, proj, conv_w)


def _xa_kernel(q_ref, k_ref, v_ref, o_ref, *, heads):
    dh = q_ref.shape[1] // heads
    for h in range(heads):
        cs = slice(h * dh, (h + 1) * dh)
        s = lax.dot_general(q_ref[:, cs], k_ref[:, cs], _NT,
                            preferred_element_type=F32) * (dh ** -0.5)
        p = jnp.exp(s - jnp.max(s, axis=-1, keepdims=True))
        l = jnp.sum(p, axis=-1, keepdims=True)
        o = jnp.dot(p.astype(BF16), v_ref[:, cs], preferred_element_type=F32)
        o_ref[:, cs] = (o / l).astype(o_ref.dtype)


def _xattn(proj, mem_kv, *, batch, seq, n_mem, width, col):
    tq = _tile(seq, 512)
    nq = seq // tq
    return pl.pallas_call(
        functools.partial(_xa_kernel, heads=XA_HEADS), grid=(batch, nq),
        in_specs=[pl.BlockSpec((tq, width), lambda b, j: (b * nq + j, col // width)),
                  pl.BlockSpec((n_mem, width), lambda b, j: (b, 0)),
                  pl.BlockSpec((n_mem, width), lambda b, j: (b, 1))],
        out_specs=pl.BlockSpec((tq, width), lambda b, j: (b * nq + j, 0)),
        out_shape=jax.ShapeDtypeStruct((batch * seq, width), BF16),
        compiler_params=_params("parallel", "parallel"),
    )(proj, mem_kv, mem_kv)


def _merge_kernel(og_ref, oc_ref, ox_ref, wg_ref, wc_ref, wx_ref, g0_ref, g1_ref, g2_ref,
                  bg_ref, o_ref):
    acc = None
    for idx, (a_ref, w_ref, g_ref) in enumerate(
            ((og_ref, wg_ref, g0_ref), (oc_ref, wc_ref, g1_ref), (ox_ref, wx_ref, g2_ref))):
        y = jnp.dot(a_ref[...], w_ref[...], preferred_element_type=F32)
        g = jax.nn.sigmoid(g_ref[...].astype(F32) + bg_ref[idx:idx + 1, :])
        acc = g * y if acc is None else acc + g * y
    o_ref[...] = acc.astype(o_ref.dtype)


def _merge(o_gla, o_conv, o_xa, w_gla, w_conv, w_xa, proj, b_gate, *, d, gate_col):
    m = o_gla.shape[0]
    tm, tn = _tile(m, 512), _tile(d, 1024)

    def lhs(a):
        return pl.BlockSpec((tm, a.shape[1]), lambda i, j: (i, 0))

    def rhs(w):
        return pl.BlockSpec((w.shape[0], tn), lambda i, j: (0, j))

    def gate(g):
        return pl.BlockSpec((tm, tn), lambda i, j: (i, (gate_col + g * d) // tn + j))

    return pl.pallas_call(
        _merge_kernel, grid=(m // tm, d // tn),
        in_specs=[lhs(o_gla), lhs(o_conv), lhs(o_xa), rhs(w_gla), rhs(w_conv), rhs(w_xa),
                  gate(0), gate(1), gate(2),
                  pl.BlockSpec((b_gate.shape[0], tn), lambda i, j: (0, j))],
        out_specs=pl.BlockSpec((tm, tn), lambda i, j: (i, j)),
        out_shape=jax.ShapeDtypeStruct((m, d), BF16),
        compiler_params=_params("parallel", "parallel"),
    )(o_gla, o_conv, o_xa, w_gla, w_conv, w_xa, proj, proj, proj, b_gate)


def _n_candidates(topk):
    return sum(topk // p for p in range(1, topk + 1))


def _route_kernel(q_ref, sk_ref, e1_ref, e2_ref, pth_ref, a_scr, b_scr, c_scr, *, topk):
    dq = sk_ref.shape[3]

    def half(c, top_scr):
        s = lax.dot_general(sk_ref[0, c], q_ref[:, c * dq:(c + 1) * dq], _NT,
                            preferred_element_type=F32)
        e = jnp.exp(s - jnp.max(s, axis=0, keepdims=True))
        top = jnp.max(e, axis=0, keepdims=True)
        top_scr[0:1, :] = top
        for r in range(1, topk):
            top = jnp.max(jnp.where(e < top, e, -1.0), axis=0, keepdims=True)
            top_scr[r:r + 1, :] = jnp.maximum(top, 0.0)
        return e

    e1 = half(0, a_scr)
    e2 = half(1, b_scr)

    def candidates(scale):
        off = 0
        for p in range(topk):
            nq = topk // (p + 1)
            c_scr[off:off + nq, :] = a_scr[p:p + 1, :] * (b_scr[0:nq, :] * scale)
            off += nq
        if off < c_scr.shape[0]:
            c_scr[off:, :] = jnp.full((c_scr.shape[0] - off, c_scr.shape[1]), -1.0, F32)
        return c_scr[...]

    cand = candidates(1.0)
    kth = jnp.max(cand, axis=0, keepdims=True)
    for r in range(1, topk):
        kth = jnp.max(jnp.where(cand < kth, cand, -1.0), axis=0, keepdims=True)
    kth = jnp.maximum(kth, 0.0)
    sel = cand >= kth
    inv_z = 1.0 / jnp.sum(jnp.where(sel, cand, 0.0), axis=0, keepdims=True)
    cand_n = candidates(inv_z)
    e1_ref[0] = e1
    e2_ref[0] = e2 * inv_z
    pth_ref[0] = jnp.min(jnp.where(sel, cand_n, jnp.inf), axis=0, keepdims=True)


def _route(q2, sub_keys):
    t = q2.shape[0]
    heads, _, nkeys, dq = sub_keys.shape
    tm = _tile(t, 512)
    ncand = -(-_n_candidates(PEER_TOPK) // 8) * 8
    fac = pl.BlockSpec((1, nkeys, tm), lambda i, h: (h, 0, i))
    return pl.pallas_call(
        functools.partial(_route_kernel, topk=PEER_TOPK), grid=(t // tm, heads),
        in_specs=[pl.BlockSpec((tm, 2 * dq), lambda i, h: (i, h)),
                  pl.BlockSpec((1, 2, nkeys, dq), lambda i, h: (h, 0, 0, 0))],
        out_specs=[fac, fac, pl.BlockSpec((1, 1, tm), lambda i, h: (h, 0, i))],
        out_shape=[jax.ShapeDtypeStruct((heads, nkeys, t), F32),
                   jax.ShapeDtypeStruct((heads, nkeys, t), F32),
                   jax.ShapeDtypeStruct((heads, 1, t), F32)],
        scratch_shapes=[pltpu.VMEM((PEER_TOPK, tm), F32), pltpu.VMEM((PEER_TOPK, tm), F32),
                        pltpu.VMEM((ncand, tm), F32)],
        compiler_params=_params("parallel", "parallel"),
    )(q2, sub_keys)


def _peer_kernel(h_ref, hs_ref, u_ref, us_ref, v_ref, vs_ref, e1_ref, e2_ref, pth_ref, o_ref,
                 w_scr, *, heads, nkeys):
    kstep = pl.program_id(1)
    tm = h_ref.shape[0]
    te = u_ref.shape[0]
    ni = te // nkeys

    @pl.when(kstep == 0)
    def _():
        o_ref[...] = jnp.zeros_like(o_ref)

    parts = SUBLANES // ni
    group = pl.ds(pl.multiple_of((kstep // parts) * SUBLANES, SUBLANES), SUBLANES)
    part = kstep % parts

    def e1_row(h, ii, ls):
        grp = e1_ref[h, group, ls]
        row = grp[ii:ii + 1]
        for s in range(1, parts):
            row = jnp.where(part == s, grp[s * ni + ii:s * ni + ii + 1], row)
        return row

    a_t = lax.dot_general(u_ref[...], h_ref[...], _NT, preferred_element_type=F32)
    for ii in range(ni):
        rs = slice(ii * nkeys, (ii + 1) * nkeys)
        u_scale = jnp.broadcast_to(us_ref[rs, :], (nkeys, LANES))
        v_scale = jnp.broadcast_to(vs_ref[rs, :], (nkeys, LANES))
        for tc in range(tm // LANES):
            ls = slice(tc * LANES, (tc + 1) * LANES)
            g = jnp.zeros((nkeys, LANES), F32)
            for h in range(heads):
                p = e1_row(h, ii, ls) * e2_ref[h, :, ls]
                g = g + jnp.where(p >= pth_ref[h, :, ls], p, 0.0)
            a = a_t[rs, ls] * (u_scale * hs_ref[:, ls])
            gelu = 0.5 * a * (1.0 + lax.erf(a * (1.0 / math.sqrt(2.0))))
            w_scr[rs, ls] = (g * gelu * v_scale).astype(w_scr.dtype)
    o_ref[...] += lax.dot_general(w_scr[...], v_ref[...], _TN, preferred_element_type=F32)


def _peer_mix(h8, h_scale, u8, u_scale, v8, v_scale, e1, e2, pth):
    t, d = h8.shape
    n_exp = u8.shape[0]
    heads, nkeys, _ = e1.shape
    tm, te = _tile(t, 512), _tile(n_exp, 512)
    fac = pl.BlockSpec((heads, nkeys, tm), lambda i, k: (0, 0, i))
    return pl.pallas_call(
        functools.partial(_peer_kernel, heads=heads, nkeys=nkeys),
        grid=(t // tm, n_exp // te),
        in_specs=[pl.BlockSpec((tm, d), lambda i, k: (i, 0)),
                  pl.BlockSpec((1, tm), lambda i, k: (0, i)),
                  pl.BlockSpec((te, d), lambda i, k: (k, 0)),
                  pl.BlockSpec((te, 1), lambda i, k: (k, 0)),
                  pl.BlockSpec((te, d), lambda i, k: (k, 0)),
                  pl.BlockSpec((te, 1), lambda i, k: (k, 0)),
                  fac, fac, pl.BlockSpec((heads, 1, tm), lambda i, k: (0, 0, i))],
        out_specs=pl.BlockSpec((tm, d), lambda i, k: (i, 0)),
        out_shape=jax.ShapeDtypeStruct((t, d), F32),
        scratch_shapes=[pltpu.VMEM((te, tm), v8.dtype)],
        compiler_params=_params("parallel", "arbitrary"),
    )(h8, h_scale, u8, u_scale, v8, v_scale, e1, e2, pth)


def kernel(x, mem, norm_mix, w_in, w_a_up, b_a, gla_norm, conv_w, w_br_gla, w_br_conv, w_mem_kv,
           w_br_xa, b_gate, w_o, mem_norm, norm_ffn, peer_wq, peer_subkeys, peer_u, peer_v,
           final_norm):
    batch, seq, d = x.shape
    n_mem = mem.shape[1]
    depth = w_in.shape[0]
    rank, qk = w_a_up.shape[1], w_a_up.shape[2]
    vw = w_br_gla.shape[1]
    cw = w_br_conv.shape[1]
    xw = w_br_xa.shape[1]
    t = batch * seq

    a_off = 2 * qk + 2 * vw
    cols_a = {"q": 0, "k": qk, "v": 2 * qk, "r": 2 * qk + vw}
    cols_b = {"cb": 0, "cc": cw, "ch": 2 * cw}
    xq_col = 3 * cw
    gate_col = xq_col + xw
    assert rank <= LANES and a_off + rank + gate_col + 3 * d == w_in.shape[2]

    xs = x.reshape(t, d)
    mem_n = _rmsnorm(mem.reshape(batch * n_mem, d), mem_norm, BF16)

    for l in range(depth):
        wup = jnp.pad(w_a_up[l], ((0, LANES - rank), (0, 0))).astype(BF16)

        hn = _rmsnorm(xs, norm_mix[l], BF16)
        w_t = jnp.swapaxes(w_in[l], 0, 1)
        proj_a = _mm(hn, _cast_rows(w_t, 0, a_off), BF16, w_is_nk=True)
        proj_b = _mm(hn, _cast_rows(w_t, a_off + rank, gate_col + 3 * d), BF16, w_is_nk=True)
        a_lr = _mm(hn, _cast_rows(w_t, a_off, LANES), BF16, w_is_nk=True)

        o_gla = _gla(proj_a, a_lr, wup, b_a[l].reshape(1, qk), gla_norm[l].reshape(1, -1),
                     batch=batch, seq=seq, qk=qk, vw=vw, cols=cols_a)
        o_conv = _conv(proj_b, conv_w[l], batch=batch, seq=seq, width=cw, cols=cols_b)
        mem_kv = _mm(mem_n, w_mem_kv[l].astype(BF16), BF16)
        o_xa = _xattn(proj_b, mem_kv, batch=batch, seq=seq, n_mem=n_mem, width=xw, col=xq_col)

        merged = _merge(o_gla, o_conv, o_xa, w_br_gla[l].astype(BF16), w_br_conv[l].astype(BF16),
                        w_br_xa[l].astype(BF16), proj_b, b_gate[l].reshape(3, d), d=d,
                        gate_col=gate_col)
        xs = _mm(merged, w_o[l].astype(BF16), F32, residual=xs)

        h2, h8, h_scale = _rmsnorm_q8(xs, norm_ffn[l])
        u8, u_scale = _quant_rows(peer_u[l], E4M3, E4M3_TARGET)
        v8, v_scale = _quant_rows(peer_v[l], E5M2, E5M2_TARGET)
        q2 = _mm(h2, peer_wq[l].astype(BF16), BF16)
        e1, e2, pth = _route(q2, peer_subkeys[l].astype(BF16))
        a_bound = d * E4M3_TARGET ** 2 * jnp.max(h_scale) * jnp.max(u_scale)
        w_scale = E5M2_TARGET / (peer_subkeys.shape[1] * a_bound * jnp.max(v_scale))
        y = _peer_mix(h8, h_scale.reshape(1, t), u8, u_scale, v8, v_scale * w_scale, e1, e2, pth)
        y_scale = 1.0 / w_scale
        if l + 1 < depth:
            xs = xs + y * y_scale
    return _rmsnorm(xs, final_norm, F32, y=y, y_scale=y_scale).reshape(batch, seq, d)
```
